```python
import jax, jax.numpy as jnp
from jax import lax
import numpy as np

D_MODEL = 4096
BATCH = 1
SEQ = 8192
DEPTH = 4

CONF_WIDTH = 768
CONF_KERNEL = 31
SC_WIDTH = 768
SC_KERNEL = 3
HEAD_DIM = 64
N_Q_HEADS = 24
N_KV_HEADS = 4
GQA_GROUP = N_Q_HEADS // N_KV_HEADS
ATTN_WIDTH = N_Q_HEADS * HEAD_DIM
KV_WIDTH = N_KV_HEADS * HEAD_DIM
WINDOW = 128
BLOCK = 128

MIX_WIDTH = CONF_WIDTH + SC_WIDTH + ATTN_WIDTH
D_FF = 4 * D_MODEL
N_GATES = 3
N_IN = 2 * CONF_WIDTH + 3 * SC_WIDTH + ATTN_WIDTH + 2 * KV_WIDTH + N_GATES * D_MODEL
DEEPNORM_ALPHA = (2 * DEPTH) ** 0.25
DEEPNORM_BETA = (8 * DEPTH) ** -0.25
LN_EPS = 1e-5

kernel_name = "hybrid_conformer_shortconv_swa_deepnorm"


def _split_points():
    sizes = [CONF_WIDTH, CONF_WIDTH, SC_WIDTH, SC_WIDTH, SC_WIDTH,
             ATTN_WIDTH, KV_WIDTH, KV_WIDTH, D_MODEL, D_MODEL, D_MODEL]
    return np.cumsum(sizes)[:-1].tolist()


def alibi_slopes(n):
    def pow2(m):
        start = 2.0 ** (-8.0 / m)
        return [start ** (i + 1) for i in range(m)]
    if (n & (n - 1)) == 0:
        s = pow2(n)
    else:
        c = 2 ** int(np.floor(np.log2(n)))
        s = pow2(c) + pow2(2 * c)[0::2][: n - c]
    return jnp.asarray(np.array(s, dtype=np.float32))


def layer_norm(x, g, b):
    x32 = x.astype(jnp.float32)
    mu = jnp.mean(x32, axis=-1, keepdims=True)
    var = jnp.mean(jnp.square(x32 - mu), axis=-1, keepdims=True)
    y = (x32 - mu) * lax.rsqrt(var + LN_EPS) * g.astype(jnp.float32) + b.astype(jnp.float32)
    return y.astype(x.dtype)


def causal_depthwise_conv(u, w):
    k, c = w.shape
    return lax.conv_general_dilated(
        u, w[:, None, :].astype(u.dtype), window_strides=(1,), padding=[(k - 1, 0)],
        dimension_numbers=('NWC', 'WIO', 'NWC'), feature_group_count=c)


def sliding_window_attention(q, k, v, sinks):
    b, s, _ = q.shape
    nb = s // BLOCK
    qb = q.reshape(b, nb, BLOCK, N_KV_HEADS, GQA_GROUP, HEAD_DIM)

    def band(t):
        t = t.reshape(b, s, N_KV_HEADS, HEAD_DIM)
        t = jnp.pad(t, ((0, 0), (BLOCK, 0), (0, 0), (0, 0))).reshape(b, nb + 1, BLOCK, N_KV_HEADS, HEAD_DIM)
        return jnp.concatenate([t[:, :-1], t[:, 1:]], axis=2)

    kb, vb = band(k), band(v)
    scores = jnp.einsum('bnqkgd,bnskd->bnkgqs', qb, kb,
                        preferred_element_type=jnp.float32) * (HEAD_DIM ** -0.5)
    qi = jnp.arange(BLOCK)[:, None]
    si = jnp.arange(2 * BLOCK)[None, :]
    dist = BLOCK + qi - si
    key_pos = (jnp.arange(nb)[:, None] - 1) * BLOCK + jnp.arange(2 * BLOCK)[None, :]
    mask = ((dist >= 0) & (dist < WINDOW))[None] & (key_pos >= 0)[:, None, :]
    slopes = alibi_slopes(N_Q_HEADS).reshape(N_KV_HEADS, GQA_GROUP)
    bias = -slopes[:, :, None, None] * dist.astype(jnp.float32)[None, None]
    scores = jnp.where(mask[None, :, None, None], scores + bias[None, None], -jnp.inf)
    sink = sinks.astype(jnp.float32).reshape(N_KV_HEADS, GQA_GROUP)[None, None, :, :, None, None]
    m = jnp.maximum(jnp.max(scores, axis=-1, keepdims=True), sink)
    e = jnp.exp(scores - m)
    probs = e / (jnp.sum(e, axis=-1, keepdims=True) + jnp.exp(sink - m))
    out = jnp.einsum('bnkgqs,bnskd->bnqkgd', probs.astype(v.dtype), vb)
    return out.reshape(b, s, ATTN_WIDTH)


def setup_inputs(seed: int = 0) -> dict:
    key = jax.random.key(seed)
    ks = jax.random.split(key, 26)
    f32 = jnp.float32

    def nrm(k, shape, scale):
        return jax.random.normal(k, shape, f32) * scale

    L = DEPTH
    w_branch = jnp.concatenate([
        nrm(ks[8], (L, CONF_WIDTH, D_MODEL), CONF_WIDTH ** -0.5 * DEEPNORM_BETA),
        nrm(ks[9], (L, SC_WIDTH, D_MODEL), SC_WIDTH ** -0.5 * DEEPNORM_BETA),
        nrm(ks[10], (L, ATTN_WIDTH, D_MODEL), ATTN_WIDTH ** -0.5 * DEEPNORM_BETA)], axis=1)
    return {
        "x": nrm(ks[0], (BATCH, SEQ, D_MODEL), 1.0),
        "w_in": nrm(ks[1], (L, D_MODEL, N_IN), D_MODEL ** -0.5),
        "b_in": nrm(ks[2], (L, N_IN), 0.02),
        "conf_dw_w": nrm(ks[3], (L, CONF_KERNEL, CONF_WIDTH), CONF_KERNEL ** -0.5),
        "conf_dw_b": nrm(ks[4], (L, CONF_WIDTH), 0.02),
        "conf_ln_g": 1.0 + nrm(ks[5], (L, CONF_WIDTH), 0.02),
        "conf_ln_b": nrm(ks[6], (L, CONF_WIDTH), 0.02),
        "sc_dw_w": nrm(ks[7], (L, SC_KERNEL, SC_WIDTH), SC_KERNEL ** -0.5),
        "attn_sinks": nrm(ks[11], (L, N_Q_HEADS), 0.5),
        "w_branch": w_branch,
        "w_out": nrm(ks[12], (L, D_MODEL, D_MODEL), D_MODEL ** -0.5 * DEEPNORM_BETA),
        "b_out": nrm(ks[13], (L, D_MODEL), 0.02),
        "ln1_g": 1.0 + nrm(ks[14], (L, D_MODEL), 0.02),
        "ln1_b": nrm(ks[15], (L, D_MODEL), 0.02),
        "w_mlp1": nrm(ks[16], (L, D_MODEL, D_FF), D_MODEL ** -0.5),
        "b_mlp1": nrm(ks[17], (L, D_FF), 0.02),
        "w_mlp2": nrm(ks[18], (L, D_FF, D_MODEL), D_FF ** -0.5 * DEEPNORM_BETA),
        "b_mlp2": nrm(ks[19], (L, D_MODEL), 0.02),
        "ln2_g": 1.0 + nrm(ks[20], (L, D_MODEL), 0.02),
        "ln2_b": nrm(ks[21], (L, D_MODEL), 0.02),
    }


def reference(x, w_in, b_in, conf_dw_w, conf_dw_b, conf_ln_g, conf_ln_b, sc_dw_w, attn_sinks,
              w_branch, w_out, b_out, ln1_g, ln1_b, w_mlp1, b_mlp1, w_mlp2, b_mlp2, ln2_g, ln2_b):
    splits = _split_points()
    a0, a1 = 0, CONF_WIDTH
    b0, b1 = CONF_WIDTH, CONF_WIDTH + SC_WIDTH
    c0 = CONF_WIDTH + SC_WIDTH
    for l in range(DEPTH):
        proj = jnp.einsum('bsd,dn->bsn', x, w_in[l]) + b_in[l]
        (conf_a, conf_g, sc_b, sc_c, sc_x, q, k, v,
         gate_a, gate_b, gate_c) = jnp.split(proj, splits, axis=-1)

        u = conf_a * jax.nn.sigmoid(conf_g)
        u = causal_depthwise_conv(u, conf_dw_w[l]) + conf_dw_b[l]
        y_a = jax.nn.silu(layer_norm(u, conf_ln_g[l], conf_ln_b[l]))

        y_b = sc_b * causal_depthwise_conv(sc_c * sc_x, sc_dw_w[l])

        y_c = sliding_window_attention(q, k, v, attn_sinks[l])

        wb = w_branch[l]
        p_a = jnp.einsum('bsc,cd->bsd', y_a, wb[a0:a1])
        p_b = jnp.einsum('bsc,cd->bsd', y_b, wb[b0:b1])
        p_c = jnp.einsum('bsc,cd->bsd', y_c, wb[c0:])
        merged = (jax.nn.sigmoid(gate_a) * p_a + jax.nn.sigmoid(gate_b) * p_b
                  + jax.nn.sigmoid(gate_c) * p_c)
        mix = jnp.einsum('bsd,de->bse', merged, w_out[l]) + b_out[l]
        x = layer_norm(DEEPNORM_ALPHA * x + mix, ln1_g[l], ln1_b[l])

        hid = jnp.square(jax.nn.relu(jnp.einsum('bsd,df->bsf', x, w_mlp1[l]) + b_mlp1[l]))
        ff = jnp.einsum('bsf,fd->bsd', hid, w_mlp2[l]) + b_mlp2[l]
        x = layer_norm(DEEPNORM_ALPHA * x + ff, ln2_g[l], ln2_b[l])
    return x
```

```python
import functools

import jax
import jax.numpy as jnp
import numpy as np
from jax import lax
from jax.experimental import pallas as pl
from jax.experimental.pallas import tpu as pltpu

F32 = jnp.float32
BF16 = jnp.bfloat16

CONF_WIDTH = 768
CONF_KERNEL = 31
SC_WIDTH = 768
SC_KERNEL = 3
HEAD_DIM = 64
N_Q_HEADS = 24
N_KV_HEADS = 4
GQA_GROUP = N_Q_HEADS // N_KV_HEADS
ATTN_WIDTH = N_Q_HEADS * HEAD_DIM
KV_WIDTH = N_KV_HEADS * HEAD_DIM
WINDOW = 128
BLOCK = 128
N_GATES = 3
LN_EPS = 1e-5
MASK_VALUE = -1e30

V7X_VMEM_BYTES = 64 * 1024 * 1024
VMEM_LIMIT_BYTES = V7X_VMEM_BYTES - 8 * 1024 * 1024

CONF_HIST = 32
CONF_CHUNK = 32
SC_HIST = 8
SC_CHUNK = 64
LN_CHUNK = 8
MM_LN_SLAB = 1024


def _alibi_slopes(n):
    def pow2(m):
        start = 2.0 ** (-8.0 / m)
        return [start ** (i + 1) for i in range(m)]
    if (n & (n - 1)) == 0:
        s = pow2(n)
    else:
        c = 2 ** int(np.floor(np.log2(n)))
        s = pow2(c) + pow2(2 * c)[0::2][: n - c]
    return np.array(s, dtype=np.float32)


def _params(n_axes):
    return pltpu.CompilerParams(
        dimension_semantics=("arbitrary",) * n_axes,
        vmem_limit_bytes=VMEM_LIMIT_BYTES,
    )


def _resident(block_shape, index_map):
    return pl.BlockSpec(block_shape, index_map, pipeline_mode=pl.Buffered(1))


def _layer_norm_rows(z, g, b):
    mu = jnp.mean(z, axis=-1, keepdims=True)
    zc = z - mu
    var = jnp.mean(zc * zc, axis=-1, keepdims=True)
    return zc * lax.rsqrt(var + LN_EPS) * g + b


def _conf_kernel(x_ref, w_ref, b_ref, dww_ref, dwb_ref, g_ref, beta_ref, o_ref, ubuf, *, tm):
    c = CONF_WIDTH

    @pl.when(pl.program_id(0) == 0)
    def _():
        ubuf[0:CONF_HIST, :] = jnp.zeros((CONF_HIST, c), F32)

    xb = x_ref[...]
    pa = jnp.dot(xb, w_ref[:, :c], preferred_element_type=F32) + b_ref[:, :c]
    pg = jnp.dot(xb, w_ref[:, c:], preferred_element_type=F32) + b_ref[:, c:]
    ubuf[CONF_HIST:CONF_HIST + tm, :] = pa * jax.nn.sigmoid(pg)

    first_tap = CONF_HIST - (CONF_KERNEL - 1)

    def chunk(ci, carry):
        base = pl.multiple_of(ci * CONF_CHUNK, CONF_CHUNK)
        win = ubuf[pl.ds(base, CONF_CHUNK + CONF_HIST), :]
        acc = jnp.broadcast_to(dwb_ref[...], (CONF_CHUNK, c))
        for k in range(CONF_KERNEL):
            acc = acc + dww_ref[k:k + 1, :] * win[first_tap + k:first_tap + k + CONF_CHUNK, :]
        y = _layer_norm_rows(acc, g_ref[...], beta_ref[...])
        o_ref[pl.ds(base, CONF_CHUNK), :] = (y * jax.nn.sigmoid(y)).astype(o_ref.dtype)
        return carry

    lax.fori_loop(0, tm // CONF_CHUNK, chunk, 0)
    ubuf[0:CONF_HIST, :] = ubuf[tm:tm + CONF_HIST, :]


def _conf_call(l, xb, w_conf, b_conf, dww, dwb, g, beta, *, tm=512):
    s, d = xb.shape
    c = CONF_WIDTH
    return pl.pallas_call(
        functools.partial(_conf_kernel, tm=tm),
        out_shape=jax.ShapeDtypeStruct((s, c), BF16),
        grid=(s // tm,),
        in_specs=[
            pl.BlockSpec((tm, d), lambda i: (i, 0)),
            _resident((None, d, 2 * c), lambda i: (l, 0, 0)),
            _resident((None, 1, 2 * c), lambda i: (l, 0, 0)),
            _resident((None, CONF_KERNEL, c), lambda i: (l, 0, 0)),
            _resident((None, 1, c), lambda i: (l, 0, 0)),
            _resident((None, 1, c), lambda i: (l, 0, 0)),
            _resident((None, 1, c), lambda i: (l, 0, 0)),
        ],
        out_specs=pl.BlockSpec((tm, c), lambda i: (i, 0)),
        scratch_shapes=[pltpu.VMEM((tm + CONF_HIST, c), F32)],
        compiler_params=_params(1),
        name="conf",
    )(xb, w_conf, b_conf, dww, dwb, g, beta)


def _sconv_kernel(x_ref, w_ref, b_ref, dww_ref, o_ref, vbuf, pbuf, *, tm):
    c = SC_WIDTH

    @pl.when(pl.program_id(0) == 0)
    def _():
        vbuf[0:SC_HIST, :] = jnp.zeros((SC_HIST, c), F32)

    xb = x_ref[...]
    pbuf[...] = jnp.dot(xb, w_ref[:, :c], preferred_element_type=F32) + b_ref[:, :c]
    pc = jnp.dot(xb, w_ref[:, c:2 * c], preferred_element_type=F32) + b_ref[:, c:2 * c]
    px = jnp.dot(xb, w_ref[:, 2 * c:], preferred_element_type=F32) + b_ref[:, 2 * c:]
    vbuf[SC_HIST:SC_HIST + tm, :] = pc * px

    first_tap = SC_HIST - (SC_KERNEL - 1)

    def chunk(ci, carry):
        base = pl.multiple_of(ci * SC_CHUNK, SC_CHUNK)
        win = vbuf[pl.ds(base, SC_CHUNK + SC_HIST), :]
        acc = dww_ref[0:1, :] * win[first_tap:first_tap + SC_CHUNK, :]
        for k in range(1, SC_KERNEL):
            acc = acc + dww_ref[k:k + 1, :] * win[first_tap + k:first_tap + k + SC_CHUNK, :]
        o_ref[pl.ds(base, SC_CHUNK), :] = (pbuf[pl.ds(base, SC_CHUNK), :] * acc).astype(o_ref.dtype)
        return carry

    lax.fori_loop(0, tm // SC_CHUNK, chunk, 0)
    vbuf[0:SC_HIST, :] = vbuf[tm:tm + SC_HIST, :]


def _sconv_call(l, xb, w_sc, b_sc, dww, *, tm=512):
    s, d = xb.shape
    c = SC_WIDTH
    return pl.pallas_call(
        functools.partial(_sconv_kernel, tm=tm),
        out_shape=jax.ShapeDtypeStruct((s, c), BF16),
        grid=(s // tm,),
        in_specs=[
            pl.BlockSpec((tm, d), lambda i: (i, 0)),
            _resident((None, d, 3 * c), lambda i: (l, 0, 0)),
            _resident((None, 1, 3 * c), lambda i: (l, 0, 0)),
            _resident((None, SC_KERNEL, c), lambda i: (l, 0, 0)),
        ],
        out_specs=pl.BlockSpec((tm, c), lambda i: (i, 0)),
        scratch_shapes=[pltpu.VMEM((tm + SC_HIST, c), F32), pltpu.VMEM((tm, c), F32)],
        compiler_params=_params(1),
        name="sconv",
    )(xb, w_sc, b_sc, dww)


def _attn_kernel(x_ref, w_ref, b_ref, sink_ref, slope_ref, o_ref, q_s, k_s, v_s, *, tm):
    aw, kw = ATTN_WIDTH, KV_WIDTH
    first = pl.program_id(0) == 0

    @pl.when(first)
    def _():
        k_s[0:BLOCK, :] = jnp.zeros((BLOCK, kw), BF16)
        v_s[0:BLOCK, :] = jnp.zeros((BLOCK, kw), BF16)

    xb = x_ref[...]
    q = (jnp.dot(xb, w_ref[:, :aw], preferred_element_type=F32) + b_ref[:, :aw]) * (HEAD_DIM ** -0.5)
    q_s[...] = q.astype(BF16)
    k = jnp.dot(xb, w_ref[:, aw:aw + kw], preferred_element_type=F32) + b_ref[:, aw:aw + kw]
    k_s[BLOCK:BLOCK + tm, :] = k.astype(BF16)
    v = jnp.dot(xb, w_ref[:, aw + kw:], preferred_element_type=F32) + b_ref[:, aw + kw:]
    v_s[BLOCK:BLOCK + tm, :] = v.astype(BF16)

    qi = lax.broadcasted_iota(jnp.int32, (BLOCK, 2 * BLOCK), 0)
    si = lax.broadcasted_iota(jnp.int32, (BLOCK, 2 * BLOCK), 1)
    dist_i = BLOCK + qi - si
    in_window = (dist_i >= 0) & (dist_i < WINDOW)
    dist = dist_i.astype(F32)

    def qblock(j, carry):
        r0 = pl.multiple_of(j * BLOCK, BLOCK)
        first_key = jnp.where(first & (j == 0), BLOCK, 0)
        mask = in_window & (si >= first_key)
        for h in range(N_KV_HEADS):
            kb = k_s[pl.ds(r0, 2 * BLOCK), h * HEAD_DIM:(h + 1) * HEAD_DIM]
            vb = v_s[pl.ds(r0, 2 * BLOCK), h * HEAD_DIM:(h + 1) * HEAD_DIM]
            for g in range(GQA_GROUP):
                hq = h * GQA_GROUP + g
                qh = q_s[pl.ds(r0, BLOCK), hq * HEAD_DIM:(hq + 1) * HEAD_DIM]
                sc = lax.dot_general(qh, kb, (((1,), (1,)), ((), ())), preferred_element_type=F32)
                sc = jnp.where(mask, sc - slope_ref[hq] * dist, MASK_VALUE)
                sink = sink_ref[hq]
                m = jnp.maximum(jnp.max(sc, axis=-1, keepdims=True), sink)
                e = jnp.exp(sc - m)
                denom = jnp.sum(e, axis=-1, keepdims=True) + jnp.exp(sink - m)
                pv = jnp.dot(e.astype(BF16), vb, preferred_element_type=F32)
                o_ref[pl.ds(r0, BLOCK), hq * HEAD_DIM:(hq + 1) * HEAD_DIM] = (pv / denom).astype(o_ref.dtype)
        return carry

    lax.fori_loop(0, tm // BLOCK, qblock, 0)
    k_s[0:BLOCK, :] = k_s[tm:tm + BLOCK, :]
    v_s[0:BLOCK, :] = v_s[tm:tm + BLOCK, :]


def _attn_call(l, xb, w_qkv, b_qkv, sinks, slopes, *, tm=512):
    s, d = xb.shape
    n = ATTN_WIDTH + 2 * KV_WIDTH
    return pl.pallas_call(
        functools.partial(_attn_kernel, tm=tm),
        out_shape=jax.ShapeDtypeStruct((s, ATTN_WIDTH), BF16),
        grid=(s // tm,),
        in_specs=[
            pl.BlockSpec((tm, d), lambda i: (i, 0)),
            _resident((None, d, n), lambda i: (l, 0, 0)),
            _resident((None, 1, n), lambda i: (l, 0, 0)),
            pl.BlockSpec(memory_space=pltpu.SMEM),
            pl.BlockSpec(memory_space=pltpu.SMEM),
        ],
        out_specs=pl.BlockSpec((tm, ATTN_WIDTH), lambda i: (i, 0)),
        scratch_shapes=[
            pltpu.VMEM((tm, ATTN_WIDTH), BF16),
            pltpu.VMEM((tm + BLOCK, KV_WIDTH), BF16),
            pltpu.VMEM((tm + BLOCK, KV_WIDTH), BF16),
        ],
        compiler_params=_params(1),
        name="attn",
    )(xb, w_qkv, b_qkv, sinks, slopes)


def _merge_kernel(x_ref, wga_ref, wgb_ref, wgc_ref, bga_ref, bgb_ref, bgc_ref,
                  ya_ref, yb_ref, yc_ref, wba_ref, wbb_ref, wbc_ref, o_ref):
    xb = x_ref[...]

    def gated(wg_ref, bg_ref, y_ref, wb_ref):
        gate = jax.nn.sigmoid(jnp.dot(xb, wg_ref[...], preferred_element_type=F32) + bg_ref[...])
        return gate * jnp.dot(y_ref[...], wb_ref[...], preferred_element_type=F32)

    merged = gated(wga_ref, bga_ref, ya_ref, wba_ref)
    merged = merged + gated(wgb_ref, bgb_ref, yb_ref, wbb_ref)
    merged = merged + gated(wgc_ref, bgc_ref, yc_ref, wbc_ref)
    o_ref[...] = merged.astype(o_ref.dtype)


def _merge_call(l, xb, w_gate, b_gate, ya, yb, yc, w_branch, *, tm=1024, tn=256):
    s, d = xb.shape
    nj = d // tn
    assert SC_WIDTH == CONF_WIDTH and ATTN_WIDTH == CONF_WIDTH + SC_WIDTH

    def wg_spec(gi):
        return pl.BlockSpec((None, d, tn), lambda i, j: (l, 0, gi * nj + j))

    def bg_spec(gi):
        return pl.BlockSpec((None, 1, tn), lambda i, j: (l, 0, gi * nj + j))

    return pl.pallas_call(
        _merge_kernel,
        out_shape=jax.ShapeDtypeStruct((s, d), BF16),
        grid=(s // tm, nj),
        in_specs=[
            pl.BlockSpec((tm, d), lambda i, j: (i, 0)),
            wg_spec(0), wg_spec(1), wg_spec(2),
            bg_spec(0), bg_spec(1), bg_spec(2),
            pl.BlockSpec((tm, CONF_WIDTH), lambda i, j: (i, 0)),
            pl.BlockSpec((tm, SC_WIDTH), lambda i, j: (i, 0)),
            pl.BlockSpec((tm, ATTN_WIDTH), lambda i, j: (i, 0)),
            pl.BlockSpec((None, CONF_WIDTH, tn), lambda i, j: (l, 0, j)),
            pl.BlockSpec((None, SC_WIDTH, tn), lambda i, j: (l, 1, j)),
            pl.BlockSpec((None, ATTN_WIDTH, tn), lambda i, j: (l, 1, j)),
        ],
        out_specs=pl.BlockSpec((tm, tn), lambda i, j: (i, j)),
        compiler_params=_params(2),
        name="merge",
    )(xb, w_gate, w_gate, w_gate, b_gate, b_gate, b_gate, ya, yb, yc, w_branch, w_branch, w_branch)


def _mm_ln_kernel(a_ref, w_ref, b_ref, res_ref, g_ref, beta_ref, of_ref, ob_ref, *, tm, nk, alpha):
    kk = pl.program_id(1)
    d = of_ref.shape[-1]

    def partial_products():
        for n0 in range(0, d, MM_LN_SLAB):
            yield n0, jnp.dot(a_ref[...], w_ref[:, n0:n0 + MM_LN_SLAB], preferred_element_type=F32)

    @pl.when(kk == 0)
    def _():
        for n0, part in partial_products():
            of_ref[:, n0:n0 + MM_LN_SLAB] = part

    @pl.when(kk > 0)
    def _():
        for n0, part in partial_products():
            of_ref[:, n0:n0 + MM_LN_SLAB] += part

    @pl.when(kk == nk - 1)
    def _():
        def chunk(ci, carry):
            rows = pl.ds(pl.multiple_of(ci * LN_CHUNK, LN_CHUNK), LN_CHUNK)
            z = alpha * res_ref[rows, :] + (of_ref[rows, :] + b_ref[...])
            y = _layer_norm_rows(z, g_ref[...], beta_ref[...])
            of_ref[rows, :] = y
            ob_ref[rows, :] = y.astype(ob_ref.dtype)
            return carry

        lax.fori_loop(0, tm // LN_CHUNK, chunk, 0)


def _mm_ln_call(l, a, w, b, res, g, beta, *, alpha, tm=512, tk=512, name):
    s, kdim = a.shape
    d = w.shape[-1]
    nk = kdim // tk
    return pl.pallas_call(
        functools.partial(_mm_ln_kernel, tm=tm, nk=nk, alpha=alpha),
        out_shape=(jax.ShapeDtypeStruct((s, d), F32), jax.ShapeDtypeStruct((s, d), BF16)),
        grid=(s // tm, nk),
        in_specs=[
            pl.BlockSpec((tm, tk), lambda i, k: (i, k)),
            pl.BlockSpec((None, tk, d), lambda i, k: (l, k, 0)),
            pl.BlockSpec((None, 1, d), lambda i, k: (l, 0, 0)),
            pl.BlockSpec((tm, d), lambda i, k: (i, 0)),
            pl.BlockSpec((None, 1, d), lambda i, k: (l, 0, 0)),
            pl.BlockSpec((None, 1, d), lambda i, k: (l, 0, 0)),
        ],
        out_specs=(pl.BlockSpec((tm, d), lambda i, k: (i, 0)), pl.BlockSpec((tm, d), lambda i, k: (i, 0))),
        compiler_params=_params(2),
        name=name,
    )(a, w, b, res, g, beta)


def _mlp1_kernel(x_ref, w_ref, b_ref, o_ref):
    h = jnp.dot(x_ref[...], w_ref[...], preferred_element_type=F32) + b_ref[...]
    h = jnp.maximum(h, 0.0)
    o_ref[...] = (h * h).astype(o_ref.dtype)


def _mlp1_call(l, xb, w, b, *, tm=1024, tn=1024):
    s, d = xb.shape
    f = w.shape[-1]
    return pl.pallas_call(
        _mlp1_kernel,
        out_shape=jax.ShapeDtypeStruct((s, f), BF16),
        grid=(s // tm, f // tn),
        in_specs=[
            pl.BlockSpec((tm, d), lambda i, j: (i, 0)),
            pl.BlockSpec((None, d, tn), lambda i, j: (l, 0, j)),
            pl.BlockSpec((None, 1, tn), lambda i, j: (l, 0, j)),
        ],
        out_specs=pl.BlockSpec((tm, tn), lambda i, j: (i, j)),
        compiler_params=_params(2),
        name="mlp1",
    )(xb, w, b)


def kernel(x, w_in, b_in, conf_dw_w, conf_dw_b, conf_ln_g, conf_ln_b, sc_dw_w, attn_sinks, w_branch, w_out, b_out,
           ln1_g, ln1_b, w_mlp1, b_mlp1, w_mlp2, b_mlp2, ln2_g, ln2_b):
    batch, seq, d_model = x.shape
    depth = w_in.shape[0]
    alpha = float((2 * depth) ** 0.25)
    assert batch == 1

    c0 = 2 * CONF_WIDTH
    c1 = c0 + 3 * SC_WIDTH
    c2 = c1 + ATTN_WIDTH + 2 * KV_WIDTH
    assert w_in.shape[-1] == c2 + N_GATES * d_model

    def row(v):
        return v[:, None, :]

    w_conf, w_sc, w_qkv, w_gate = (w_in[:, :, :c0].astype(BF16), w_in[:, :, c0:c1].astype(BF16),
                                   w_in[:, :, c1:c2].astype(BF16), w_in[:, :, c2:].astype(BF16))
    b_conf, b_sc, b_qkv, b_gate = row(b_in[:, :c0]), row(b_in[:, c0:c1]), row(b_in[:, c1:c2]), row(b_in[:, c2:])
    w_branch_b = w_branch.astype(BF16)
    w_out_b = w_out.astype(BF16)
    w_mlp1_b = w_mlp1.astype(BF16)
    w_mlp2_b = w_mlp2.astype(BF16)
    slopes = jnp.asarray(_alibi_slopes(N_Q_HEADS))

    xf = x.reshape(seq, d_model)
    xb = xf.astype(BF16)
    for l in range(depth):
        ya = _conf_call(l, xb, w_conf, b_conf, conf_dw_w, row(conf_dw_b), row(conf_ln_g), row(conf_ln_b))
        yb = _sconv_call(l, xb, w_sc, b_sc, sc_dw_w)
        yc = _attn_call(l, xb, w_qkv, b_qkv, attn_sinks[l], slopes)
        merged = _merge_call(l, xb, w_gate, b_gate, ya, yb, yc, w_branch_b)
        xf, xb = _mm_ln_call(l, merged, w_out_b, row(b_out), xf, row(ln1_g), row(ln1_b), alpha=alpha, name="out_ln")
        hid = _mlp1_call(l, xb, w_mlp1_b, row(b_mlp1))
        xf, xb = _mm_ln_call(l, hid, w_mlp2_b, row(b_mlp2), xf, row(ln2_g), row(ln2_b), alpha=alpha, name="mlp2_ln")
    return xf.reshape(batch, seq, d_model)
```

```python
import functools

import jax
import jax.numpy as jnp
import numpy as np
from jax import lax
from jax.experimental import pallas as pl
from jax.experimental.pallas import tpu as pltpu

F32 = jnp.float32
BF16 = jnp.bfloat16

CONF_WIDTH = 768
CONF_KERNEL = 31
SC_WIDTH = 768
SC_KERNEL = 3
HEAD_DIM = 64
N_Q_HEADS = 24
N_KV_HEADS = 4
GQA_GROUP = N_Q_HEADS // N_KV_HEADS
ATTN_WIDTH = N_Q_HEADS * HEAD_DIM
KV_WIDTH = N_KV_HEADS * HEAD_DIM
WINDOW = 128
BLOCK = 128
N_GATES = 3
LN_EPS = 1e-5
MASK_VALUE = -1e30

COL_CONF = 0
COL_SC = COL_CONF + 2 * CONF_WIDTH
COL_Q = COL_SC + 3 * SC_WIDTH
COL_K = COL_Q + ATTN_WIDTH
COL_V = COL_K + KV_WIDTH
COL_GATE = COL_V + KV_WIDTH

SUBLANES = 8

V7X_VMEM_BYTES = 64 * 1024 * 1024
VMEM_LIMIT_BYTES = V7X_VMEM_BYTES - 8 * 1024 * 1024

CONF_HIST = 32
CONF_CHUNK = 32
SC_HIST = 8
SC_CHUNK = 64
LN_CHUNK = 8
MM_LN_SLAB = 1024


def _alibi_slopes(n):
    def pow2(m):
        start = 2.0 ** (-8.0 / m)
        return [start ** (i + 1) for i in range(m)]
    if (n & (n - 1)) == 0:
        s = pow2(n)
    else:
        c = 2 ** int(np.floor(np.log2(n)))
        s = pow2(c) + pow2(2 * c)[0::2][: n - c]
    return np.array(s, dtype=np.float32)


def _params(n_axes):
    return pltpu.CompilerParams(
        dimension_semantics=("arbitrary",) * n_axes,
        vmem_limit_bytes=VMEM_LIMIT_BYTES,
    )


def _resident(block_shape, index_map):
    return pl.BlockSpec(block_shape, index_map, pipeline_mode=pl.Buffered(1))


def _in_cols(l, d, col, width, *, resident):
    assert col % width == 0
    make = _resident if resident else pl.BlockSpec
    return (make((None, d, width), lambda *_: (l, 0, col // width)),
            make((None, 1, width), lambda *_: (l, 0, col // width)))


def _layer_norm_rows(z, g, b):
    mu = jnp.mean(z, axis=-1, keepdims=True)
    zc = z - mu
    var = jnp.mean(zc * zc, axis=-1, keepdims=True)
    return zc * lax.rsqrt(var + LN_EPS) * g + b


def _ln_row_slab(load_z, first_row, g_ref, beta_ref, of_ref, ob_ref):
    for c0 in range(0, of_ref.shape[0], LN_CHUNK):
        y = _layer_norm_rows(load_z(pl.ds(first_row + c0, LN_CHUNK)), g_ref[...], beta_ref[...])
        of_ref[c0:c0 + LN_CHUNK, :] = y
        ob_ref[c0:c0 + LN_CHUNK, :] = y.astype(ob_ref.dtype)


def _pipelined_steps(i, n_tiles, matmul_step, ln_step):
    @pl.when(i == 0)
    def _():
        matmul_step()

    @pl.when((i > 0) & (i < n_tiles))
    def _():
        ln_step()
        matmul_step()

    @pl.when(i == n_tiles)
    def _():
        ln_step()


def _conf_kernel(x_ref, w_ref, b_ref, dww_ref, dwb_ref, g_ref, beta_ref, o_ref, ubuf, wtap, *, tm):
    c = CONF_WIDTH

    @pl.when(pl.program_id(0) == 0)
    def _():
        for s in range(SUBLANES):
            ubuf[s, 0:CONF_HIST, :] = jnp.zeros((CONF_HIST, c), F32)
        for k in range(CONF_KERNEL):
            wtap[k] = jnp.broadcast_to(dww_ref[k:k + 1, :], (SUBLANES, c))

    xb = x_ref[...]
    pa = jnp.dot(xb, w_ref[:, :c], preferred_element_type=F32) + b_ref[:, :c]
    pg = jnp.dot(xb, w_ref[:, c:], preferred_element_type=F32) + b_ref[:, c:]
    u = pa * jax.nn.sigmoid(pg)
    for s in range(SUBLANES):
        ubuf[s, CONF_HIST - s:CONF_HIST - s + tm, :] = u

    first_tap = CONF_HIST - (CONF_KERNEL - 1)

    def chunk(ci, carry):
        base = pl.multiple_of(ci * CONF_CHUNK, CONF_CHUNK)
        acc = jnp.broadcast_to(dwb_ref[...], (CONF_CHUNK, c))
        for k in range(CONF_KERNEL):
            s = (first_tap + k) % SUBLANES
            off = first_tap + k - s
            wk = jnp.concatenate([wtap[k]] * (CONF_CHUNK // SUBLANES), axis=0)
            acc = acc + wk * ubuf[s, pl.ds(base + off, CONF_CHUNK), :]
        y = _layer_norm_rows(acc, g_ref[...], beta_ref[...])
        o_ref[pl.ds(base, CONF_CHUNK), :] = (y * jax.nn.sigmoid(y)).astype(o_ref.dtype)
        return carry

    lax.fori_loop(0, tm // CONF_CHUNK, chunk, 0, unroll=2)
    for s in range(SUBLANES):
        ubuf[s, 0:CONF_HIST, :] = ubuf[s, tm:tm + CONF_HIST, :]


def _conf_call(l, xb, w_in, b_in, dww, dwb, g, beta, *, tm=512):
    s, d = xb.shape
    c = CONF_WIDTH
    return pl.pallas_call(
        functools.partial(_conf_kernel, tm=tm),
        out_shape=jax.ShapeDtypeStruct((s, c), BF16),
        grid=(s // tm,),
        in_specs=[
            pl.BlockSpec((tm, d), lambda i: (i, 0)),
            *_in_cols(l, d, COL_CONF, 2 * c, resident=True),
            _resident((None, CONF_KERNEL, c), lambda i: (l, 0, 0)),
            _resident((None, 1, c), lambda i: (l, 0, 0)),
            _resident((None, 1, c), lambda i: (l, 0, 0)),
            _resident((None, 1, c), lambda i: (l, 0, 0)),
        ],
        out_specs=pl.BlockSpec((tm, c), lambda i: (i, 0)),
        scratch_shapes=[pltpu.VMEM((SUBLANES, tm + CONF_HIST, c), F32),
                        pltpu.VMEM((CONF_KERNEL, SUBLANES, c), F32)],
        compiler_params=_params(1),
        name="conf",
    )(xb, w_in, b_in, dww, dwb, g, beta)


def _sconv_kernel(x_ref, wb_ref, bb_ref, wc_ref, bc_ref, wx_ref, bx_ref, dww_ref, o_ref, vbuf, pbuf, *, tm):
    c = SC_WIDTH

    @pl.when(pl.program_id(0) == 0)
    def _():
        vbuf[0:SC_HIST, :] = jnp.zeros((SC_HIST, c), F32)

    xb = x_ref[...]
    pbuf[...] = jnp.dot(xb, wb_ref[...], preferred_element_type=F32) + bb_ref[...]
    pc = jnp.dot(xb, wc_ref[...], preferred_element_type=F32) + bc_ref[...]
    px = jnp.dot(xb, wx_ref[...], preferred_element_type=F32) + bx_ref[...]
    vbuf[SC_HIST:SC_HIST + tm, :] = pc * px

    first_tap = SC_HIST - (SC_KERNEL - 1)

    def chunk(ci, carry):
        base = pl.multiple_of(ci * SC_CHUNK, SC_CHUNK)
        win = vbuf[pl.ds(base, SC_CHUNK + SC_HIST), :]
        acc = dww_ref[0:1, :] * win[first_tap:first_tap + SC_CHUNK, :]
        for k in range(1, SC_KERNEL):
            acc = acc + dww_ref[k:k + 1, :] * win[first_tap + k:first_tap + k + SC_CHUNK, :]
        o_ref[pl.ds(base, SC_CHUNK), :] = (pbuf[pl.ds(base, SC_CHUNK), :] * acc).astype(o_ref.dtype)
        return carry

    lax.fori_loop(0, tm // SC_CHUNK, chunk, 0)
    vbuf[0:SC_HIST, :] = vbuf[tm:tm + SC_HIST, :]


def _sconv_call(l, xb, w_in, b_in, dww, *, tm=512):
    s, d = xb.shape
    c = SC_WIDTH
    wb, bb = _in_cols(l, d, COL_SC, c, resident=True)
    wc, bc = _in_cols(l, d, COL_SC + c, c, resident=True)
    wx, bx = _in_cols(l, d, COL_SC + 2 * c, c, resident=True)
    return pl.pallas_call(
        functools.partial(_sconv_kernel, tm=tm),
        out_shape=jax.ShapeDtypeStruct((s, c), BF16),
        grid=(s // tm,),
        in_specs=[
            pl.BlockSpec((tm, d), lambda i: (i, 0)),
            wb, bb, wc, bc, wx, bx,
            _resident((None, SC_KERNEL, c), lambda i: (l, 0, 0)),
        ],
        out_specs=pl.BlockSpec((tm, c), lambda i: (i, 0)),
        scratch_shapes=[pltpu.VMEM((tm + SC_HIST, c), F32), pltpu.VMEM((tm, c), F32)],
        compiler_params=_params(1),
        name="sconv",
    )(xb, w_in, b_in, w_in, b_in, w_in, b_in, dww)


def _attn_kernel(x_ref, wq0_ref, bq0_ref, wq1_ref, bq1_ref, wk_ref, bk_ref, wv_ref, bv_ref, sink_ref, slope_ref,
                 o_ref, q_s, k_s, v_s, *, tm):
    kw = KV_WIDTH
    half = ATTN_WIDTH // 2
    first = pl.program_id(0) == 0

    @pl.when(first)
    def _():
        k_s[0:BLOCK, :] = jnp.zeros((BLOCK, kw), BF16)
        v_s[0:BLOCK, :] = jnp.zeros((BLOCK, kw), BF16)

    xb = x_ref[...]
    scale = HEAD_DIM ** -0.5
    q0 = (jnp.dot(xb, wq0_ref[...], preferred_element_type=F32) + bq0_ref[...]) * scale
    q_s[:, :half] = q0.astype(BF16)
    q1 = (jnp.dot(xb, wq1_ref[...], preferred_element_type=F32) + bq1_ref[...]) * scale
    q_s[:, half:] = q1.astype(BF16)
    k = jnp.dot(xb, wk_ref[...], preferred_element_type=F32) + bk_ref[...]
    k_s[BLOCK:BLOCK + tm, :] = k.astype(BF16)
    v = jnp.dot(xb, wv_ref[...], preferred_element_type=F32) + bv_ref[...]
    v_s[BLOCK:BLOCK + tm, :] = v.astype(BF16)

    qi = lax.broadcasted_iota(jnp.int32, (BLOCK, 2 * BLOCK), 0)
    si = lax.broadcasted_iota(jnp.int32, (BLOCK, 2 * BLOCK), 1)
    dist_i = BLOCK + qi - si
    in_window = (dist_i >= 0) & (dist_i < WINDOW)
    dist = dist_i.astype(F32)

    def qblock(j, carry):
        r0 = pl.multiple_of(j * BLOCK, BLOCK)
        first_key = jnp.where(first & (j == 0), BLOCK, 0)
        mask = in_window & (si >= first_key)
        for h in range(N_KV_HEADS):
            kb = k_s[pl.ds(r0, 2 * BLOCK), h * HEAD_DIM:(h + 1) * HEAD_DIM]
            vb = v_s[pl.ds(r0, 2 * BLOCK), h * HEAD_DIM:(h + 1) * HEAD_DIM]
            for g in range(GQA_GROUP):
                hq = h * GQA_GROUP + g
                qh = q_s[pl.ds(r0, BLOCK), hq * HEAD_DIM:(hq + 1) * HEAD_DIM]
                sc = lax.dot_general(qh, kb, (((1,), (1,)), ((), ())), preferred_element_type=F32)
                sc = jnp.where(mask, sc - slope_ref[hq] * dist, MASK_VALUE)
                sink = sink_ref[hq]
                m = jnp.maximum(jnp.max(sc, axis=-1, keepdims=True), sink)
                e = jnp.exp(sc - m)
                denom = jnp.sum(e, axis=-1, keepdims=True) + jnp.exp(sink - m)
                pv = jnp.dot(e.astype(BF16), vb, preferred_element_type=F32)
                o_ref[pl.ds(r0, BLOCK), hq * HEAD_DIM:(hq + 1) * HEAD_DIM] = (pv / denom).astype(o_ref.dtype)
        return carry

    lax.fori_loop(0, tm // BLOCK, qblock, 0)
    k_s[0:BLOCK, :] = k_s[tm:tm + BLOCK, :]
    v_s[0:BLOCK, :] = v_s[tm:tm + BLOCK, :]


def _attn_call(l, xb, w_in, b_in, sinks, slopes, *, tm=512):
    s, d = xb.shape
    half = ATTN_WIDTH // 2
    wq0, bq0 = _in_cols(l, d, COL_Q, half, resident=True)
    wq1, bq1 = _in_cols(l, d, COL_Q + half, half, resident=True)
    wk, bk = _in_cols(l, d, COL_K, KV_WIDTH, resident=True)
    wv, bv = _in_cols(l, d, COL_V, KV_WIDTH, resident=True)
    return pl.pallas_call(
        functools.partial(_attn_kernel, tm=tm),
        out_shape=jax.ShapeDtypeStruct((s, ATTN_WIDTH), BF16),
        grid=(s // tm,),
        in_specs=[
            pl.BlockSpec((tm, d), lambda i: (i, 0)),
            wq0, bq0, wq1, bq1, wk, bk, wv, bv,
            pl.BlockSpec(memory_space=pltpu.SMEM),
            pl.BlockSpec(memory_space=pltpu.SMEM),
        ],
        out_specs=pl.BlockSpec((tm, ATTN_WIDTH), lambda i: (i, 0)),
        scratch_shapes=[
            pltpu.VMEM((tm, ATTN_WIDTH), BF16),
            pltpu.VMEM((tm + BLOCK, KV_WIDTH), BF16),
            pltpu.VMEM((tm + BLOCK, KV_WIDTH), BF16),
        ],
        compiler_params=_params(1),
        name="attn",
    )(xb, w_in, b_in, w_in, b_in, w_in, b_in, w_in, b_in, sinks, slopes)


def _merge_kernel(x_ref, wga_ref, wgb_ref, wgc_ref, bga_ref, bgb_ref, bgc_ref,
                  ya_ref, yb_ref, yc_ref, wba_ref, wbb_ref, wbc_ref, o_ref):
    xb = x_ref[...]

    def gated(wg_ref, bg_ref, y_ref, wb_ref):
        gate = jax.nn.sigmoid(jnp.dot(xb, wg_ref[...], preferred_element_type=F32) + bg_ref[...])
        return gate * jnp.dot(y_ref[...], wb_ref[...], preferred_element_type=F32)

    merged = gated(wga_ref, bga_ref, ya_ref, wba_ref)
    merged = merged + gated(wgb_ref, bgb_ref, yb_ref, wbb_ref)
    merged = merged + gated(wgc_ref, bgc_ref, yc_ref, wbc_ref)
    o_ref[...] = merged.astype(o_ref.dtype)


def _merge_call(l, xb, w_in, b_in, ya, yb, yc, w_branch, *, tm=1024, tn=256):
    s, d = xb.shape
    nj = d // tn
    assert COL_GATE % tn == 0
    gate0 = COL_GATE // tn
    assert SC_WIDTH == CONF_WIDTH and ATTN_WIDTH == CONF_WIDTH + SC_WIDTH

    def wg_spec(gi):
        return pl.BlockSpec((None, d, tn), lambda i, j: (l, 0, gate0 + gi * nj + j))

    def bg_spec(gi):
        return pl.BlockSpec((None, 1, tn), lambda i, j: (l, 0, gate0 + gi * nj + j))

    return pl.pallas_call(
        _merge_kernel,
        out_shape=jax.ShapeDtypeStruct((s, d), BF16),
        grid=(s // tm, nj),
        in_specs=[
            pl.BlockSpec((tm, d), lambda i, j: (i, 0)),
            wg_spec(0), wg_spec(1), wg_spec(2),
            bg_spec(0), bg_spec(1), bg_spec(2),
            pl.BlockSpec((tm, CONF_WIDTH), lambda i, j: (i, 0)),
            pl.BlockSpec((tm, SC_WIDTH), lambda i, j: (i, 0)),
            pl.BlockSpec((tm, ATTN_WIDTH), lambda i, j: (i, 0)),
            pl.BlockSpec((None, CONF_WIDTH, tn), lambda i, j: (l, 0, j)),
            pl.BlockSpec((None, SC_WIDTH, tn), lambda i, j: (l, 1, j)),
            pl.BlockSpec((None, ATTN_WIDTH, tn), lambda i, j: (l, 1, j)),
        ],
        out_specs=pl.BlockSpec((tm, tn), lambda i, j: (i, j)),
        compiler_params=_params(2),
        name="merge",
    )(xb, w_in, w_in, w_in, b_in, b_in, b_in, ya, yb, yc, w_branch, w_branch, w_branch)


def _out_ln_kernel(a_ref, w_ref, b_ref, res_ref, g_ref, beta_ref, of_ref, ob_ref, z_s, *, nn, n_tiles, alpha):
    i = pl.program_id(0)
    j = pl.program_id(1)
    cur = i % 2
    slab_rows = of_ref.shape[0]

    def matmul_step():
        z = jnp.dot(a_ref[...], w_ref[...], preferred_element_type=F32) + b_ref[...]
        z_s[cur, j] = z + alpha * res_ref[...]

    def ln_step():
        def load_z(rows):
            return jnp.concatenate([z_s[1 - cur, jj, rows, :] for jj in range(nn)], axis=-1)

        _ln_row_slab(load_z, pl.multiple_of(j * slab_rows, slab_rows), g_ref, beta_ref, of_ref, ob_ref)

    _pipelined_steps(i, n_tiles, matmul_step, ln_step)


def _out_ln_call(l, a, w, b, res, g, beta, *, alpha, tm=512, tn=512):
    s, kdim = a.shape
    d = w.shape[-1]
    tn = min(tn, d)
    nn = d // tn
    n_tiles = s // tm
    slab_rows = tm // nn
    assert slab_rows % (2 * SUBLANES) == 0

    def tile(i):
        return jnp.minimum(i, n_tiles - 1)

    def col(i, j):
        return jnp.where(i == n_tiles, nn - 1, j)

    def out_slab(i, j):
        return (jnp.where(i == 0, 0, (i - 1) * nn + j), 0)

    return pl.pallas_call(
        functools.partial(_out_ln_kernel, nn=nn, n_tiles=n_tiles, alpha=alpha),
        out_shape=(jax.ShapeDtypeStruct((s, d), F32), jax.ShapeDtypeStruct((s, d), BF16)),
        grid=(n_tiles + 1, nn),
        in_specs=[
            pl.BlockSpec((tm, kdim), lambda i, j: (tile(i), 0)),
            pl.BlockSpec((None, kdim, tn), lambda i, j: (l, 0, col(i, j))),
            pl.BlockSpec((None, 1, tn), lambda i, j: (l, 0, col(i, j))),
            pl.BlockSpec((tm, tn), lambda i, j: (tile(i), col(i, j))),
            pl.BlockSpec((None, 1, d), lambda i, j: (l, 0, 0)),
            pl.BlockSpec((None, 1, d), lambda i, j: (l, 0, 0)),
        ],
        out_specs=(pl.BlockSpec((slab_rows, d), out_slab), pl.BlockSpec((slab_rows, d), out_slab)),
        scratch_shapes=[pltpu.VMEM((2, nn, tm, tn), F32)],
        compiler_params=_params(2),
        name="out_ln",
    )(a, w, b, res, g, beta)


def _mlp2_ln_kernel(a_ref, w_ref, b_ref, res_ref, g_ref, beta_ref, of_ref, ob_ref, z_s, *, nk, n_tiles, alpha):
    i = pl.program_id(0)
    kk = pl.program_id(1)
    cur = i % 2
    d = z_s.shape[-1]
    slab_rows = of_ref.shape[0]
    slab = pl.ds(pl.multiple_of(kk * slab_rows, slab_rows), slab_rows)
    slab_w = min(MM_LN_SLAB, d)

    def matmul_step():
        for n0 in range(0, d, slab_w):
            part = jnp.dot(a_ref[...], w_ref[:, n0:n0 + slab_w], preferred_element_type=F32)
            z_s[cur, :, n0:n0 + slab_w] = jnp.where(kk == 0, part, z_s[cur, :, n0:n0 + slab_w] + part)
        z_s[cur, slab, :] += alpha * res_ref[...] + b_ref[...]

    def ln_step():
        _ln_row_slab(lambda rows: z_s[1 - cur, rows, :], pl.multiple_of(kk * slab_rows, slab_rows),
                     g_ref, beta_ref, of_ref, ob_ref)

    @pl.when((i == 0) & (kk == 0))
    def _():
        z_s[...] = jnp.zeros(z_s.shape, F32)

    _pipelined_steps(i, n_tiles, matmul_step, ln_step)


def _mlp2_ln_call(l, a, w, b, res, g, beta, *, alpha, tm=512, tk=1024):
    s, kdim = a.shape
    d = w.shape[-1]
    nk = kdim // tk
    n_tiles = s // tm
    slab_rows = tm // nk
    assert tm % nk == 0 and slab_rows % (2 * SUBLANES) == 0

    def tile(i):
        return jnp.minimum(i, n_tiles - 1)

    def kstep(i, k):
        return jnp.where(i == n_tiles, nk - 1, k)

    def out_slab(i, k):
        return (jnp.where(i == 0, 0, (i - 1) * nk + k), 0)

    return pl.pallas_call(
        functools.partial(_mlp2_ln_kernel, nk=nk, n_tiles=n_tiles, alpha=alpha),
        out_shape=(jax.ShapeDtypeStruct((s, d), F32), jax.ShapeDtypeStruct((s, d), BF16)),
        grid=(n_tiles + 1, nk),
        in_specs=[
            pl.BlockSpec((tm, tk), lambda i, k: (tile(i), kstep(i, k))),
            pl.BlockSpec((None, tk, d), lambda i, k: (l, kstep(i, k), 0)),
            pl.BlockSpec((None, 1, d), lambda i, k: (l, 0, 0)),
            pl.BlockSpec((slab_rows, d), lambda i, k: (tile(i) * nk + kstep(i, k), 0)),
            pl.BlockSpec((None, 1, d), lambda i, k: (l, 0, 0)),
            pl.BlockSpec((None, 1, d), lambda i, k: (l, 0, 0)),
        ],
        out_specs=(pl.BlockSpec((slab_rows, d), out_slab), pl.BlockSpec((slab_rows, d), out_slab)),
        scratch_shapes=[pltpu.VMEM((2, tm, d), F32)],
        compiler_params=_params(2),
        name="mlp2_ln",
    )(a, w, b, res, g, beta)


def _mlp1_kernel(x_ref, w_ref, b_ref, o_ref):
    h = jnp.dot(x_ref[...], w_ref[...].astype(BF16), preferred_element_type=F32) + b_ref[...]
    h = jnp.maximum(h, 0.0)
    o_ref[...] = (h * h).astype(o_ref.dtype)


def _mlp1_call(l, xb, w, b, *, tm=1024, tn=512):
    s, d = xb.shape
    f = w.shape[-1]
    return pl.pallas_call(
        _mlp1_kernel,
        out_shape=jax.ShapeDtypeStruct((s, f), BF16),
        grid=(s // tm, f // tn),
        in_specs=[
            pl.BlockSpec((tm, d), lambda i, j: (i, 0)),
            pl.BlockSpec((None, d, tn), lambda i, j: (l, 0, j)),
            pl.BlockSpec((None, 1, tn), lambda i, j: (l, 0, j)),
        ],
        out_specs=pl.BlockSpec((tm, tn), lambda i, j: (i, j)),
        compiler_params=_params(2),
        name="mlp1",
    )(xb, w, b)


def kernel(x, w_in, b_in, conf_dw_w, conf_dw_b, conf_ln_g, conf_ln_b, sc_dw_w, attn_sinks, w_branch, w_out, b_out,
           ln1_g, ln1_b, w_mlp1, b_mlp1, w_mlp2, b_mlp2, ln2_g, ln2_b):
    batch, seq, d_model = x.shape
    depth = w_in.shape[0]
    alpha = float((2 * depth) ** 0.25)
    assert batch == 1
    assert w_in.shape[-1] == COL_GATE + N_GATES * d_model

    def row(v):
        return v[:, None, :]

    w_in_b = w_in.astype(BF16)
    w_branch_b = w_branch.astype(BF16)
    w_out_b = w_out.astype(BF16)
    w_mlp2_b = w_mlp2.astype(BF16)
    b_in_r = row(b_in)
    slopes = jnp.asarray(_alibi_slopes(N_Q_HEADS))

    xf = x.reshape(seq, d_model)
    xb = xf.astype(BF16)
    for l in range(depth):
        ya = _conf_call(l, xb, w_in_b, b_in_r, conf_dw_w, row(conf_dw_b), row(conf_ln_g), row(conf_ln_b))
        yb = _sconv_call(l, xb, w_in_b, b_in_r, sc_dw_w)
        yc = _attn_call(l, xb, w_in_b, b_in_r, attn_sinks[l], slopes)
        merged = _merge_call(l, xb, w_in_b, b_in_r, ya, yb, yc, w_branch_b)
        xf, xb = _out_ln_call(l, merged, w_out_b, row(b_out), xf, row(ln1_g), row(ln1_b), alpha=alpha)
        hid = _mlp1_call(l, xb, w_mlp1, row(b_mlp1))
        xf, xb = _mlp2_ln_call(l, hid, w_mlp2_b, row(b_mlp2), xf, row(ln2_g), row(ln2_b), alpha=alpha)
    return xf.reshape(batch, seq, d_model)
```

```python
import functools

import jax
import jax.numpy as jnp
import numpy as np
from jax import lax
from jax.experimental import pallas as pl
from jax.experimental.pallas import tpu as pltpu

F32 = jnp.float32
BF16 = jnp.bfloat16

CONF_WIDTH = 768
CONF_KERNEL = 31
SC_WIDTH = 768
SC_KERNEL = 3
HEAD_DIM = 64
N_Q_HEADS = 24
N_KV_HEADS = 4
GQA_GROUP = N_Q_HEADS // N_KV_HEADS
ATTN_WIDTH = N_Q_HEADS * HEAD_DIM
KV_WIDTH = N_KV_HEADS * HEAD_DIM
WINDOW = 128
BLOCK = 128
N_GATES = 3
LN_EPS = 1e-5

COL_CONF = 0
COL_SC = COL_CONF + 2 * CONF_WIDTH
COL_Q = COL_SC + 3 * SC_WIDTH
COL_K = COL_Q + ATTN_WIDTH
COL_V = COL_K + KV_WIDTH
COL_GATE = COL_V + KV_WIDTH

SUBLANES = 8

V7X_VMEM_BYTES = 64 * 1024 * 1024
VMEM_LIMIT_BYTES = V7X_VMEM_BYTES - 8 * 1024 * 1024

CONF_HIST = 32
CONF_CHUNK = 32
CONF_COLS = 256
SC_HIST = 8
SC_CHUNK = 64
LN_CHUNK = 8
MM_LN_SLAB = 1024


def _alibi_slopes(n):
    def pow2(m):
        start = 2.0 ** (-8.0 / m)
        return [start ** (i + 1) for i in range(m)]
    if (n & (n - 1)) == 0:
        s = pow2(n)
    else:
        c = 2 ** int(np.floor(np.log2(n)))
        s = pow2(c) + pow2(2 * c)[0::2][: n - c]
    return np.array(s, dtype=np.float32)


def _params(n_axes):
    return pltpu.CompilerParams(
        dimension_semantics=("arbitrary",) * n_axes,
        vmem_limit_bytes=VMEM_LIMIT_BYTES,
    )


def _resident(block_shape, index_map):
    return pl.BlockSpec(block_shape, index_map, pipeline_mode=pl.Buffered(1))


def _in_cols(l, d, col, width, *, resident):
    assert col % width == 0
    make = _resident if resident else pl.BlockSpec
    return (make((None, d, width), lambda *_: (l, 0, col // width)),
            make((None, 1, width), lambda *_: (l, 0, col // width)))


def _layer_norm_rows(z, g, b):
    mu = jnp.mean(z, axis=-1, keepdims=True)
    zc = z - mu
    var = jnp.mean(zc * zc, axis=-1, keepdims=True)
    return zc * lax.rsqrt(var + LN_EPS) * g + b


def _ln_row_slab(load_z, first_row, g_ref, beta_ref, of_ref, ob_ref):
    for c0 in range(0, of_ref.shape[0], LN_CHUNK):
        y = _layer_norm_rows(load_z(pl.ds(first_row + c0, LN_CHUNK)), g_ref[...], beta_ref[...])
        of_ref[c0:c0 + LN_CHUNK, :] = y
        ob_ref[c0:c0 + LN_CHUNK, :] = y.astype(ob_ref.dtype)


def _pipelined_steps(i, n_tiles, produce, consume):
    @pl.when(i == 0)
    def _():
        produce(0)

    for par in (0, 1):
        @pl.when((i > 0) & (i < n_tiles) & (i % 2 == par))
        def _():
            consume(1 - par)
            produce(par)

    @pl.when(i == n_tiles)
    def _():
        consume((n_tiles - 1) % 2)


def _conf_kernel(x_ref, w_ref, b_ref, dww_ref, dwb_ref, g_ref, beta_ref, o_ref, ubuf0, ubuf1, wtap, *, tm, n_tiles):
    c = CONF_WIDTH
    i = pl.program_id(0)
    ubufs = (ubuf0, ubuf1)

    @pl.when(i == 0)
    def _():
        for ubuf in ubufs:
            ubuf[...] = jnp.zeros(ubuf.shape, F32)
        for k in range(CONF_KERNEL):
            wtap[k] = jnp.broadcast_to(dww_ref[k:k + 1, :], (SUBLANES, c))

    def project(par):
        cur, prev = ubufs[par], ubufs[1 - par]
        xb = x_ref[...]
        for c0 in range(0, c, CONF_COLS):
            cols = slice(c0, c0 + CONF_COLS)
            gcols = slice(c + c0, c + c0 + CONF_COLS)
            pa = jnp.dot(xb, w_ref[:, cols], preferred_element_type=F32) + b_ref[:, cols]
            pg = jnp.dot(xb, w_ref[:, gcols], preferred_element_type=F32) + b_ref[:, gcols]
            u = pa * jax.nn.sigmoid(pg)
            for s in range(SUBLANES):
                cur[s, 0:CONF_HIST, cols] = prev[s, tm:tm + CONF_HIST, cols]
                cur[s, CONF_HIST - s:CONF_HIST - s + tm, cols] = u

    first_tap = CONF_HIST - (CONF_KERNEL - 1)

    def convolve(par):
        ubuf = ubufs[par]
        for base in range(0, tm, CONF_CHUNK):
            acc = jnp.broadcast_to(dwb_ref[...], (CONF_CHUNK, c))
            for k in range(CONF_KERNEL):
                s = (first_tap + k) % SUBLANES
                off = first_tap + k - s
                wk = jnp.concatenate([wtap[k]] * (CONF_CHUNK // SUBLANES), axis=0)
                acc = acc + wk * ubuf[s, base + off:base + off + CONF_CHUNK, :]
            y = _layer_norm_rows(acc, g_ref[...], beta_ref[...])
            o_ref[base:base + CONF_CHUNK, :] = (y * jax.nn.sigmoid(y)).astype(o_ref.dtype)

    _pipelined_steps(i, n_tiles, project, convolve)


def _conf_call(l, xb, w_in, b_in, dww, dwb, g, beta, *, tm=256):
    s, d = xb.shape
    c = CONF_WIDTH
    n_tiles = s // tm
    return pl.pallas_call(
        functools.partial(_conf_kernel, tm=tm, n_tiles=n_tiles),
        out_shape=jax.ShapeDtypeStruct((s, c), BF16),
        grid=(n_tiles + 1,),
        in_specs=[
            pl.BlockSpec((tm, d), lambda i: (jnp.minimum(i, n_tiles - 1), 0)),
            *_in_cols(l, d, COL_CONF, 2 * c, resident=True),
            _resident((None, CONF_KERNEL, c), lambda i: (l, 0, 0)),
            _resident((None, 1, c), lambda i: (l, 0, 0)),
            _resident((None, 1, c), lambda i: (l, 0, 0)),
            _resident((None, 1, c), lambda i: (l, 0, 0)),
        ],
        out_specs=pl.BlockSpec((tm, c), lambda i: (jnp.maximum(i - 1, 0), 0)),
        scratch_shapes=[pltpu.VMEM((SUBLANES, tm + CONF_HIST, c), F32),
                        pltpu.VMEM((SUBLANES, tm + CONF_HIST, c), F32),
                        pltpu.VMEM((CONF_KERNEL, SUBLANES, c), F32)],
        compiler_params=_params(1),
        name="conf",
    )(xb, w_in, b_in, dww, dwb, g, beta)


def _sconv_kernel(x_ref, wb_ref, bb_ref, wc_ref, bc_ref, wx_ref, bx_ref, dww_ref, o_ref, vbuf, pbuf, *, tm):
    c = SC_WIDTH

    @pl.when(pl.program_id(0) == 0)
    def _():
        vbuf[0:SC_HIST, :] = jnp.zeros((SC_HIST, c), F32)

    xb = x_ref[...]
    pbuf[...] = jnp.dot(xb, wb_ref[...], preferred_element_type=F32) + bb_ref[...]
    pc = jnp.dot(xb, wc_ref[...], preferred_element_type=F32) + bc_ref[...]
    px = jnp.dot(xb, wx_ref[...], preferred_element_type=F32) + bx_ref[...]
    vbuf[SC_HIST:SC_HIST + tm, :] = pc * px

    first_tap = SC_HIST - (SC_KERNEL - 1)

    def chunk(ci, carry):
        base = pl.multiple_of(ci * SC_CHUNK, SC_CHUNK)
        win = vbuf[pl.ds(base, SC_CHUNK + SC_HIST), :]
        acc = dww_ref[0:1, :] * win[first_tap:first_tap + SC_CHUNK, :]
        for k in range(1, SC_KERNEL):
            acc = acc + dww_ref[k:k + 1, :] * win[first_tap + k:first_tap + k + SC_CHUNK, :]
        o_ref[pl.ds(base, SC_CHUNK), :] = (pbuf[pl.ds(base, SC_CHUNK), :] * acc).astype(o_ref.dtype)
        return carry

    lax.fori_loop(0, tm // SC_CHUNK, chunk, 0)
    vbuf[0:SC_HIST, :] = vbuf[tm:tm + SC_HIST, :]


def _sconv_call(l, xb, w_in, b_in, dww, *, tm=512):
    s, d = xb.shape
    c = SC_WIDTH
    wb, bb = _in_cols(l, d, COL_SC, c, resident=True)
    wc, bc = _in_cols(l, d, COL_SC + c, c, resident=True)
    wx, bx = _in_cols(l, d, COL_SC + 2 * c, c, resident=True)
    return pl.pallas_call(
        functools.partial(_sconv_kernel, tm=tm),
        out_shape=jax.ShapeDtypeStruct((s, c), BF16),
        grid=(s // tm,),
        in_specs=[
            pl.BlockSpec((tm, d), lambda i: (i, 0)),
            wb, bb, wc, bc, wx, bx,
            _resident((None, SC_KERNEL, c), lambda i: (l, 0, 0)),
        ],
        out_specs=pl.BlockSpec((tm, c), lambda i: (i, 0)),
        scratch_shapes=[pltpu.VMEM((tm + SC_HIST, c), F32), pltpu.VMEM((tm, c), F32)],
        compiler_params=_params(1),
        name="sconv",
    )(xb, w_in, b_in, w_in, b_in, w_in, b_in, dww)


def _attn_kernel(x_ref, wq0_ref, bq0_ref, wq1_ref, bq1_ref, wk_ref, bk_ref, wv_ref, bv_ref, sink_ref, slope_ref,
                 o_ref, q_s, klo_s, khi_s, vlo_s, vhi_s, bias_s, *, tm, n_tiles):
    i = pl.program_id(0)
    half_w = ATTN_WIDTH // 2
    slab = 2 * HEAD_DIM
    q_slabs_per_kv = GQA_GROUP // 2
    low_lanes = lax.broadcasted_iota(jnp.int32, (tm, slab), 1) < HEAD_DIM
    low_lanes_block = lax.broadcasted_iota(jnp.int32, (BLOCK, slab), 1) < HEAD_DIM

    qi = lax.broadcasted_iota(jnp.int32, (BLOCK, 2 * BLOCK), 0)
    si = lax.broadcasted_iota(jnp.int32, (BLOCK, 2 * BLOCK), 1)
    dist_i = BLOCK + qi - si
    in_window = (dist_i >= 0) & (dist_i < WINDOW)

    @pl.when(i == 0)
    def _():
        dist = dist_i.astype(F32)
        for hq in range(N_Q_HEADS):
            bias_s[hq] = -slope_ref[hq] * dist
        for ref in (klo_s, khi_s, vlo_s, vhi_s):
            ref[1, tm:tm + BLOCK, :] = jnp.zeros((BLOCK, N_KV_HEADS * slab), BF16)

    def split_heads(t, lo_ref, hi_ref, cur):
        zero = jnp.zeros((tm, slab), F32)
        for p in range(N_KV_HEADS // 2):
            pair = t[:, p * slab:(p + 1) * slab]
            swapped = pltpu.roll(pair, HEAD_DIM, axis=1)
            even, odd = 2 * p, 2 * p + 1
            lo_ref[cur, BLOCK:BLOCK + tm, even * slab:(even + 1) * slab] = jnp.where(low_lanes, pair, zero).astype(BF16)
            hi_ref[cur, BLOCK:BLOCK + tm, even * slab:(even + 1) * slab] = jnp.where(low_lanes, zero, swapped).astype(BF16)
            lo_ref[cur, BLOCK:BLOCK + tm, odd * slab:(odd + 1) * slab] = jnp.where(low_lanes, swapped, zero).astype(BF16)
            hi_ref[cur, BLOCK:BLOCK + tm, odd * slab:(odd + 1) * slab] = jnp.where(low_lanes, zero, pair).astype(BF16)

    def project(cur):
        prev = 1 - cur
        xb = x_ref[...]
        scale = HEAD_DIM ** -0.5
        q0 = (jnp.dot(xb, wq0_ref[...], preferred_element_type=F32) + bq0_ref[...]) * scale
        q_s[cur, :, :half_w] = q0.astype(BF16)
        q1 = (jnp.dot(xb, wq1_ref[...], preferred_element_type=F32) + bq1_ref[...]) * scale
        q_s[cur, :, half_w:] = q1.astype(BF16)
        split_heads(jnp.dot(xb, wk_ref[...], preferred_element_type=F32) + bk_ref[...], klo_s, khi_s, cur)
        split_heads(jnp.dot(xb, wv_ref[...], preferred_element_type=F32) + bv_ref[...], vlo_s, vhi_s, cur)
        for ref in (klo_s, khi_s, vlo_s, vhi_s):
            ref[cur, 0:BLOCK, :] = ref[prev, tm:tm + BLOCK, :]

    def attend(prev):
        for jb in range(tm // BLOCK):
            r0 = jb * BLOCK
            if jb == 0:
                mask = in_window & (si >= jnp.where(i == 1, BLOCK, 0))
            else:
                mask = in_window
            for h in range(N_KV_HEADS):
                cols = slice(h * slab, (h + 1) * slab)
                band = slice(r0, r0 + 2 * BLOCK)
                kcat = jnp.concatenate([klo_s[prev, band, cols], khi_s[prev, band, cols]], axis=0)
                vcat = jnp.concatenate([vlo_s[prev, band, cols], vhi_s[prev, band, cols]], axis=0)
                first_slab = h * q_slabs_per_kv
                qst = jnp.concatenate(
                    [q_s[prev, r0:r0 + BLOCK, (first_slab + s) * slab:(first_slab + s + 1) * slab]
                     for s in range(q_slabs_per_kv)], axis=0)
                scores = lax.dot_general(qst, kcat, (((1,), (1,)), ((), ())), preferred_element_type=F32)
                e_rows, den_rows = [], []
                for s in range(q_slabs_per_kv):
                    e_cols, dens = [], []
                    for odd in range(2):
                        hq = 2 * (first_slab + s) + odd
                        blk = scores[s * BLOCK:(s + 1) * BLOCK, odd * 2 * BLOCK:(odd + 1) * 2 * BLOCK]
                        sc = jnp.where(mask, blk + bias_s[hq], -jnp.inf)
                        sink = sink_ref[hq]
                        m = jnp.maximum(jnp.max(sc, axis=-1, keepdims=True), sink)
                        e = jnp.exp(sc - m)
                        dens.append(jnp.sum(e, axis=-1, keepdims=True) + jnp.exp(sink - m))
                        e_cols.append(e.astype(BF16))
                    e_rows.append(jnp.concatenate(e_cols, axis=1))
                    den_rows.append(jnp.where(low_lanes_block, dens[0], dens[1]))
                pv = jnp.dot(jnp.concatenate(e_rows, axis=0), vcat, preferred_element_type=F32)
                out = pv / jnp.concatenate(den_rows, axis=0)
                for s in range(q_slabs_per_kv):
                    o_ref[r0:r0 + BLOCK, (first_slab + s) * slab:(first_slab + s + 1) * slab] = (
                        out[s * BLOCK:(s + 1) * BLOCK, :].astype(o_ref.dtype))

    _pipelined_steps(i, n_tiles, project, attend)


def _attn_call(l, xb, w_in, b_in, sinks, slopes, *, tm=256):
    s, d = xb.shape
    half = ATTN_WIDTH // 2
    n_tiles = s // tm
    assert GQA_GROUP % 2 == 0 and N_KV_HEADS % 2 == 0
    wq0, bq0 = _in_cols(l, d, COL_Q, half, resident=True)
    wq1, bq1 = _in_cols(l, d, COL_Q + half, half, resident=True)
    wk, bk = _in_cols(l, d, COL_K, KV_WIDTH, resident=True)
    wv, bv = _in_cols(l, d, COL_V, KV_WIDTH, resident=True)
    kv_scratch = pltpu.VMEM((2, tm + BLOCK, N_KV_HEADS * 2 * HEAD_DIM), BF16)
    return pl.pallas_call(
        functools.partial(_attn_kernel, tm=tm, n_tiles=n_tiles),
        out_shape=jax.ShapeDtypeStruct((s, ATTN_WIDTH), BF16),
        grid=(n_tiles + 1,),
        in_specs=[
            pl.BlockSpec((tm, d), lambda i: (jnp.minimum(i, n_tiles - 1), 0)),
            wq0, bq0, wq1, bq1, wk, bk, wv, bv,
            pl.BlockSpec(memory_space=pltpu.SMEM),
            pl.BlockSpec(memory_space=pltpu.SMEM),
        ],
        out_specs=pl.BlockSpec((tm, ATTN_WIDTH), lambda i: (jnp.maximum(i - 1, 0), 0)),
        scratch_shapes=[
            pltpu.VMEM((2, tm, ATTN_WIDTH), BF16),
            kv_scratch, kv_scratch, kv_scratch, kv_scratch,
            pltpu.VMEM((N_Q_HEADS, BLOCK, 2 * BLOCK), F32),
        ],
        compiler_params=_params(1),
        name="attn",
    )(xb, w_in, b_in, w_in, b_in, w_in, b_in, w_in, b_in, sinks, slopes)


def _merge_kernel(x_ref, wga_ref, wgb_ref, wgc_ref, bga_ref, bgb_ref, bgc_ref,
                  ya_ref, yb_ref, yc_ref, wba_ref, wbb_ref, wbc_ref, o_ref):
    xb = x_ref[...]

    def gated(wg_ref, bg_ref, y_ref, wb_ref):
        gate = jax.nn.sigmoid(jnp.dot(xb, wg_ref[...], preferred_element_type=F32) + bg_ref[...])
        return gate * jnp.dot(y_ref[...], wb_ref[...], preferred_element_type=F32)

    merged = gated(wga_ref, bga_ref, ya_ref, wba_ref)
    merged = merged + gated(wgb_ref, bgb_ref, yb_ref, wbb_ref)
    merged = merged + gated(wgc_ref, bgc_ref, yc_ref, wbc_ref)
    o_ref[...] = merged.astype(o_ref.dtype)


def _merge_call(l, xb, w_in, b_in, ya, yb, yc, w_branch, *, tm=1024, tn=256):
    s, d = xb.shape
    nj = d // tn
    assert COL_GATE % tn == 0
    gate0 = COL_GATE // tn
    assert SC_WIDTH == CONF_WIDTH and ATTN_WIDTH == CONF_WIDTH + SC_WIDTH

    def wg_spec(gi):
        return pl.BlockSpec((None, d, tn), lambda i, j: (l, 0, gate0 + gi * nj + j))

    def bg_spec(gi):
        return pl.BlockSpec((None, 1, tn), lambda i, j: (l, 0, gate0 + gi * nj + j))

    return pl.pallas_call(
        _merge_kernel,
        out_shape=jax.ShapeDtypeStruct((s, d), BF16),
        grid=(s // tm, nj),
        in_specs=[
            pl.BlockSpec((tm, d), lambda i, j: (i, 0)),
            wg_spec(0), wg_spec(1), wg_spec(2),
            bg_spec(0), bg_spec(1), bg_spec(2),
            pl.BlockSpec((tm, CONF_WIDTH), lambda i, j: (i, 0)),
            pl.BlockSpec((tm, SC_WIDTH), lambda i, j: (i, 0)),
            pl.BlockSpec((tm, ATTN_WIDTH), lambda i, j: (i, 0)),
            pl.BlockSpec((None, CONF_WIDTH, tn), lambda i, j: (l, 0, j)),
            pl.BlockSpec((None, SC_WIDTH, tn), lambda i, j: (l, 1, j)),
            pl.BlockSpec((None, ATTN_WIDTH, tn), lambda i, j: (l, 1, j)),
        ],
        out_specs=pl.BlockSpec((tm, tn), lambda i, j: (i, j)),
        compiler_params=_params(2),
        name="merge",
    )(xb, w_in, w_in, w_in, b_in, b_in, b_in, ya, yb, yc, w_branch, w_branch, w_branch)


def _out_ln_kernel(a_ref, w_ref, b_ref, res_ref, g_ref, beta_ref, of_ref, ob_ref, z_s, *, nn, n_tiles, alpha):
    i = pl.program_id(0)
    j = pl.program_id(1)
    slab_rows = of_ref.shape[0]

    def matmul_step(par):
        z = jnp.dot(a_ref[...], w_ref[...], preferred_element_type=F32) + b_ref[...]
        z_s[par, j] = z + alpha * res_ref[...]

    def ln_step(par):
        def load_z(rows):
            return jnp.concatenate([z_s[par, jj, rows, :] for jj in range(nn)], axis=-1)

        _ln_row_slab(load_z, pl.multiple_of(j * slab_rows, slab_rows), g_ref, beta_ref, of_ref, ob_ref)

    _pipelined_steps(i, n_tiles, matmul_step, ln_step)


def _out_ln_call(l, a, w, b, res, g, beta, *, alpha, tm=512, tn=512):
    s, kdim = a.shape
    d = w.shape[-1]
    tn = min(tn, d)
    nn = d // tn
    n_tiles = s // tm
    slab_rows = tm // nn
    assert slab_rows % (2 * SUBLANES) == 0

    def tile(i):
        return jnp.minimum(i, n_tiles - 1)

    def col(i, j):
        return jnp.where(i == n_tiles, nn - 1, j)

    def out_slab(i, j):
        return (jnp.where(i == 0, 0, (i - 1) * nn + j), 0)

    return pl.pallas_call(
        functools.partial(_out_ln_kernel, nn=nn, n_tiles=n_tiles, alpha=alpha),
        out_shape=(jax.ShapeDtypeStruct((s, d), F32), jax.ShapeDtypeStruct((s, d), BF16)),
        grid=(n_tiles + 1, nn),
        in_specs=[
            pl.BlockSpec((tm, kdim), lambda i, j: (tile(i), 0)),
            pl.BlockSpec((None, kdim, tn), lambda i, j: (l, 0, col(i, j))),
            pl.BlockSpec((None, 1, tn), lambda i, j: (l, 0, col(i, j))),
            pl.BlockSpec((tm, tn), lambda i, j: (tile(i), col(i, j))),
            pl.BlockSpec((None, 1, d), lambda i, j: (l, 0, 0)),
            pl.BlockSpec((None, 1, d), lambda i, j: (l, 0, 0)),
        ],
        out_specs=(pl.BlockSpec((slab_rows, d), out_slab), pl.BlockSpec((slab_rows, d), out_slab)),
        scratch_shapes=[pltpu.VMEM((2, nn, tm, tn), F32)],
        compiler_params=_params(2),
        name="out_ln",
    )(a, w, b, res, g, beta)


def _mlp2_ln_kernel(a_ref, w_ref, b_ref, res_ref, g_ref, beta_ref, of_ref, ob_ref, z_s, *, nk, n_tiles, alpha):
    i = pl.program_id(0)
    kk = pl.program_id(1)
    d = z_s.shape[-1]
    slab_rows = of_ref.shape[0]
    slab = pl.ds(pl.multiple_of(kk * slab_rows, slab_rows), slab_rows)
    slab_w = min(MM_LN_SLAB, d)

    def matmul_step(par):
        for n0 in range(0, d, slab_w):
            part = jnp.dot(a_ref[...], w_ref[:, n0:n0 + slab_w], preferred_element_type=F32)
            z_s[par, :, n0:n0 + slab_w] = jnp.where(kk == 0, part, z_s[par, :, n0:n0 + slab_w] + part)
        z_s[par, slab, :] += alpha * res_ref[...] + b_ref[...]

    def ln_step(par):
        _ln_row_slab(lambda rows: z_s[par, rows, :], pl.multiple_of(kk * slab_rows, slab_rows),
                     g_ref, beta_ref, of_ref, ob_ref)

    @pl.when((i == 0) & (kk == 0))
    def _():
        z_s[...] = jnp.zeros(z_s.shape, F32)

    _pipelined_steps(i, n_tiles, matmul_step, ln_step)


def _mlp2_ln_call(l, a, w, b, res, g, beta, *, alpha, tm=512, tk=1024):
    s, kdim = a.shape
    d = w.shape[-1]
    nk = kdim // tk
    n_tiles = s // tm
    slab_rows = tm // nk
    assert tm % nk == 0 and slab_rows % (2 * SUBLANES) == 0

    def tile(i):
        return jnp.minimum(i, n_tiles - 1)

    def kstep(i, k):
        return jnp.where(i == n_tiles, nk - 1, k)

    def out_slab(i, k):
        return (jnp.where(i == 0, 0, (i - 1) * nk + k), 0)

    return pl.pallas_call(
        functools.partial(_mlp2_ln_kernel, nk=nk, n_tiles=n_tiles, alpha=alpha),
        out_shape=(jax.ShapeDtypeStruct((s, d), F32), jax.ShapeDtypeStruct((s, d), BF16)),
        grid=(n_tiles + 1, nk),
        in_specs=[
            pl.BlockSpec((tm, tk), lambda i, k: (tile(i), kstep(i, k))),
            pl.BlockSpec((None, tk, d), lambda i, k: (l, kstep(i, k), 0)),
            pl.BlockSpec((None, 1, d), lambda i, k: (l, 0, 0)),
            pl.BlockSpec((slab_rows, d), lambda i, k: (tile(i) * nk + kstep(i, k), 0)),
            pl.BlockSpec((None, 1, d), lambda i, k: (l, 0, 0)),
            pl.BlockSpec((None, 1, d), lambda i, k: (l, 0, 0)),
        ],
        out_specs=(pl.BlockSpec((slab_rows, d), out_slab), pl.BlockSpec((slab_rows, d), out_slab)),
        scratch_shapes=[pltpu.VMEM((2, tm, d), F32)],
        compiler_params=_params(2),
        name="mlp2_ln",
    )(a, w, b, res, g, beta)


def _mlp1_kernel(x_ref, w_ref, b_ref, o_ref):
    h = jnp.dot(x_ref[...], w_ref[...].astype(BF16), preferred_element_type=F32) + b_ref[...]
    h = jnp.maximum(h, 0.0)
    o_ref[...] = (h * h).astype(o_ref.dtype)


def _mlp1_call(l, xb, w, b, *, tm=1024, tn=512):
    s, d = xb.shape
    f = w.shape[-1]
    return pl.pallas_call(
        _mlp1_kernel,
        out_shape=jax.ShapeDtypeStruct((s, f), BF16),
        grid=(s // tm, f // tn),
        in_specs=[
            pl.BlockSpec((tm, d), lambda i, j: (i, 0)),
            pl.BlockSpec((None, d, tn), lambda i, j: (l, 0, j)),
            pl.BlockSpec((None, 1, tn), lambda i, j: (l, 0, j)),
        ],
        out_specs=pl.BlockSpec((tm, tn), lambda i, j: (i, j)),
        compiler_params=_params(2),
        name="mlp1",
    )(xb, w, b)


def kernel(x, w_in, b_in, conf_dw_w, conf_dw_b, conf_ln_g, conf_ln_b, sc_dw_w, attn_sinks, w_branch, w_out, b_out,
           ln1_g, ln1_b, w_mlp1, b_mlp1, w_mlp2, b_mlp2, ln2_g, ln2_b):
    batch, seq, d_model = x.shape
    depth = w_in.shape[0]
    alpha = float((2 * depth) ** 0.25)
    assert batch == 1
    assert w_in.shape[-1] == COL_GATE + N_GATES * d_model

    def row(v):
        return v[:, None, :]

    w_in_b = w_in.astype(BF16)
    w_branch_b = w_branch.astype(BF16)
    w_out_b = w_out.astype(BF16)
    w_mlp2_b = w_mlp2.astype(BF16)
    b_in_r = row(b_in)
    slopes = jnp.asarray(_alibi_slopes(N_Q_HEADS))

    xf = x.reshape(seq, d_model)
    xb = xf.astype(BF16)
    for l in range(depth):
        ya = _conf_call(l, xb, w_in_b, b_in_r, conf_dw_w, row(conf_dw_b), row(conf_ln_g), row(conf_ln_b))
        yb = _sconv_call(l, xb, w_in_b, b_in_r, sc_dw_w)
        yc = _attn_call(l, xb, w_in_b, b_in_r, attn_sinks[l], slopes)
        merged = _merge_call(l, xb, w_in_b, b_in_r, ya, yb, yc, w_branch_b)
        xf, xb = _out_ln_call(l, merged, w_out_b, row(b_out), xf, row(ln1_g), row(ln1_b), alpha=alpha)
        hid = _mlp1_call(l, xb, w_mlp1, row(b_mlp1))
        xf, xb = _mlp2_ln_call(l, hid, w_mlp2_b, row(b_mlp2), xf, row(ln2_g), row(ln2_b), alpha=alpha)
    return xf.reshape(batch, seq, d_model)
```

```python
import functools

import jax
import jax.numpy as jnp
import numpy as np
from jax import lax
from jax.experimental import pallas as pl
from jax.experimental.pallas import tpu as pltpu

F32 = jnp.float32
BF16 = jnp.bfloat16

CONF_WIDTH = 768
CONF_KERNEL = 31
SC_WIDTH = 768
SC_KERNEL = 3
HEAD_DIM = 64
N_Q_HEADS = 24
N_KV_HEADS = 4
GQA_GROUP = N_Q_HEADS // N_KV_HEADS
ATTN_WIDTH = N_Q_HEADS * HEAD_DIM
KV_WIDTH = N_KV_HEADS * HEAD_DIM
MIX_WIDTH = CONF_WIDTH + SC_WIDTH + ATTN_WIDTH
WINDOW = 128
BLOCK = 128
N_GATES = 3
LN_EPS = 1e-5

COL_CONF = 0
COL_SC = COL_CONF + 2 * CONF_WIDTH
COL_Q = COL_SC + 3 * SC_WIDTH
COL_K = COL_Q + ATTN_WIDTH
COL_V = COL_K + KV_WIDTH
COL_GATE = COL_V + KV_WIDTH

SUBLANES = 8

V7X_VMEM_BYTES = 64 * 1024 * 1024
VMEM_LIMIT_BYTES = V7X_VMEM_BYTES - 4 * 1024 * 1024

CONF_HIST = 32
CONF_CHUNK = 32
CONF_COLS = 256
SC_HIST = 8
SC_CHUNK = 64
LN_CHUNK = 8
MM_LN_SLAB = 1024
OUT_LN_SLAB = 1024


def _alibi_slopes(n):
    def pow2(m):
        start = 2.0 ** (-8.0 / m)
        return [start ** (i + 1) for i in range(m)]
    if (n & (n - 1)) == 0:
        s = pow2(n)
    else:
        c = 2 ** int(np.floor(np.log2(n)))
        s = pow2(c) + pow2(2 * c)[0::2][: n - c]
    return np.array(s, dtype=np.float32)


def _params(n_axes):
    return pltpu.CompilerParams(
        dimension_semantics=("arbitrary",) * n_axes,
        vmem_limit_bytes=VMEM_LIMIT_BYTES,
    )


def _resident(block_shape, index_map):
    return pl.BlockSpec(block_shape, index_map, pipeline_mode=pl.Buffered(1))


def _in_cols(l, d, col, width, *, resident):
    assert col % width == 0
    make = _resident if resident else pl.BlockSpec
    return (make((None, d, width), lambda *_: (l, 0, col // width)),
            make((None, 1, width), lambda *_: (l, 0, col // width)))


def _layer_norm_rows(z, g, b):
    mu = jnp.mean(z, axis=-1, keepdims=True)
    zc = z - mu
    var = jnp.mean(zc * zc, axis=-1, keepdims=True)
    return zc * lax.rsqrt(var + LN_EPS) * g + b


def _ln_row_slab(load_z, first_row, g_ref, beta_ref, of_ref, ob_ref):
    for c0 in range(0, of_ref.shape[0], LN_CHUNK):
        y = _layer_norm_rows(load_z(pl.ds(first_row + c0, LN_CHUNK)), g_ref[...], beta_ref[...])
        of_ref[c0:c0 + LN_CHUNK, :] = y
        ob_ref[c0:c0 + LN_CHUNK, :] = y.astype(ob_ref.dtype)


def _pipelined_steps(i, n_tiles, produce, consume):
    @pl.when(i == 0)
    def _():
        produce(0)

    for par in (0, 1):
        @pl.when((i > 0) & (i < n_tiles) & (i % 2 == par))
        def _():
            consume(1 - par)
            produce(par)

    @pl.when(i == n_tiles)
    def _():
        consume((n_tiles - 1) % 2)


def _conf_kernel(x_ref, w_ref, b_ref, dww_ref, dwb_ref, g_ref, beta_ref, o_ref, ubuf0, ubuf1, wtap, *, tm, n_tiles):
    c = CONF_WIDTH
    i = pl.program_id(0)
    ubufs = (ubuf0, ubuf1)

    @pl.when(i == 0)
    def _():
        for ubuf in ubufs:
            ubuf[...] = jnp.zeros(ubuf.shape, F32)
        for k in range(CONF_KERNEL):
            wtap[k] = jnp.broadcast_to(dww_ref[k:k + 1, :], (SUBLANES, c))

    def project(par):
        cur, prev = ubufs[par], ubufs[1 - par]
        xb = x_ref[...]
        for c0 in range(0, c, CONF_COLS):
            cols = slice(c0, c0 + CONF_COLS)
            gcols = slice(c + c0, c + c0 + CONF_COLS)
            pa = jnp.dot(xb, w_ref[:, cols], preferred_element_type=F32) + b_ref[:, cols]
            pg = jnp.dot(xb, w_ref[:, gcols], preferred_element_type=F32) + b_ref[:, gcols]
            u = pa * jax.nn.sigmoid(pg)
            for s in range(SUBLANES):
                cur[s, 0:CONF_HIST, cols] = prev[s, tm:tm + CONF_HIST, cols]
                cur[s, CONF_HIST - s:CONF_HIST - s + tm, cols] = u

    first_tap = CONF_HIST - (CONF_KERNEL - 1)

    def convolve(par):
        ubuf = ubufs[par]
        for base in range(0, tm, CONF_CHUNK):
            acc = jnp.broadcast_to(dwb_ref[...], (CONF_CHUNK, c))
            for k in range(CONF_KERNEL):
                s = (first_tap + k) % SUBLANES
                off = first_tap + k - s
                wk = jnp.concatenate([wtap[k]] * (CONF_CHUNK // SUBLANES), axis=0)
                acc = acc + wk * ubuf[s, base + off:base + off + CONF_CHUNK, :]
            y = _layer_norm_rows(acc, g_ref[...], beta_ref[...])
            o_ref[base:base + CONF_CHUNK, :] = (y * jax.nn.sigmoid(y)).astype(o_ref.dtype)

    _pipelined_steps(i, n_tiles, project, convolve)


def _conf_call(l, xb, w_in, b_in, dww, dwb, g, beta, *, tm=256):
    s, d = xb.shape
    c = CONF_WIDTH
    n_tiles = s // tm
    return pl.pallas_call(
        functools.partial(_conf_kernel, tm=tm, n_tiles=n_tiles),
        out_shape=jax.ShapeDtypeStruct((s, c), BF16),
        grid=(n_tiles + 1,),
        in_specs=[
            pl.BlockSpec((tm, d), lambda i: (jnp.minimum(i, n_tiles - 1), 0)),
            *_in_cols(l, d, COL_CONF, 2 * c, resident=True),
            _resident((None, CONF_KERNEL, c), lambda i: (l, 0, 0)),
            _resident((None, 1, c), lambda i: (l, 0, 0)),
            _resident((None, 1, c), lambda i: (l, 0, 0)),
            _resident((None, 1, c), lambda i: (l, 0, 0)),
        ],
        out_specs=pl.BlockSpec((tm, c), lambda i: (jnp.maximum(i - 1, 0), 0)),
        scratch_shapes=[pltpu.VMEM((SUBLANES, tm + CONF_HIST, c), F32),
                        pltpu.VMEM((SUBLANES, tm + CONF_HIST, c), F32),
                        pltpu.VMEM((CONF_KERNEL, SUBLANES, c), F32)],
        compiler_params=_params(1),
        name="conf",
    )(xb, w_in, b_in, dww, dwb, g, beta)


def _sconv_kernel(x_ref, wb_ref, bb_ref, wc_ref, bc_ref, wx_ref, bx_ref, dww_ref, o_ref, vbuf, pbuf, *, tm):
    c = SC_WIDTH

    @pl.when(pl.program_id(0) == 0)
    def _():
        vbuf[0:SC_HIST, :] = jnp.zeros((SC_HIST, c), F32)

    xb = x_ref[...]
    pbuf[...] = jnp.dot(xb, wb_ref[...], preferred_element_type=F32) + bb_ref[...]
    pc = jnp.dot(xb, wc_ref[...], preferred_element_type=F32) + bc_ref[...]
    px = jnp.dot(xb, wx_ref[...], preferred_element_type=F32) + bx_ref[...]
    vbuf[SC_HIST:SC_HIST + tm, :] = pc * px

    first_tap = SC_HIST - (SC_KERNEL - 1)

    def chunk(ci, carry):
        base = pl.multiple_of(ci * SC_CHUNK, SC_CHUNK)
        win = vbuf[pl.ds(base, SC_CHUNK + SC_HIST), :]
        acc = dww_ref[0:1, :] * win[first_tap:first_tap + SC_CHUNK, :]
        for k in range(1, SC_KERNEL):
            acc = acc + dww_ref[k:k + 1, :] * win[first_tap + k:first_tap + k + SC_CHUNK, :]
        o_ref[pl.ds(base, SC_CHUNK), :] = (pbuf[pl.ds(base, SC_CHUNK), :] * acc).astype(o_ref.dtype)
        return carry

    lax.fori_loop(0, tm // SC_CHUNK, chunk, 0)
    vbuf[0:SC_HIST, :] = vbuf[tm:tm + SC_HIST, :]


def _sconv_call(l, xb, w_in, b_in, dww, *, tm=512):
    s, d = xb.shape
    c = SC_WIDTH
    wb, bb = _in_cols(l, d, COL_SC, c, resident=True)
    wc, bc = _in_cols(l, d, COL_SC + c, c, resident=True)
    wx, bx = _in_cols(l, d, COL_SC + 2 * c, c, resident=True)
    return pl.pallas_call(
        functools.partial(_sconv_kernel, tm=tm),
        out_shape=jax.ShapeDtypeStruct((s, c), BF16),
        grid=(s // tm,),
        in_specs=[
            pl.BlockSpec((tm, d), lambda i: (i, 0)),
            wb, bb, wc, bc, wx, bx,
            _resident((None, SC_KERNEL, c), lambda i: (l, 0, 0)),
        ],
        out_specs=pl.BlockSpec((tm, c), lambda i: (i, 0)),
        scratch_shapes=[pltpu.VMEM((tm + SC_HIST, c), F32), pltpu.VMEM((tm, c), F32)],
        compiler_params=_params(1),
        name="sconv",
    )(xb, w_in, b_in, w_in, b_in, w_in, b_in, dww)


def _attn_kernel(x_ref, wq0_ref, bq0_ref, wq1_ref, bq1_ref, wk_ref, bk_ref, wv_ref, bv_ref, sink_ref, slope_ref,
                 o_ref, q_s, klo_s, khi_s, vlo_s, vhi_s, bias_s, *, tm, n_tiles):
    i = pl.program_id(0)
    half_w = ATTN_WIDTH // 2
    slab = 2 * HEAD_DIM
    q_slabs_per_kv = GQA_GROUP // 2
    low_lanes = lax.broadcasted_iota(jnp.int32, (tm, slab), 1) < HEAD_DIM
    low_lanes_block = lax.broadcasted_iota(jnp.int32, (BLOCK, slab), 1) < HEAD_DIM

    qi = lax.broadcasted_iota(jnp.int32, (BLOCK, 2 * BLOCK), 0)
    si = lax.broadcasted_iota(jnp.int32, (BLOCK, 2 * BLOCK), 1)
    dist_i = BLOCK + qi - si
    in_window = (dist_i >= 0) & (dist_i < WINDOW)

    @pl.when(i == 0)
    def _():
        dist = dist_i.astype(F32)
        for hq in range(N_Q_HEADS):
            bias_s[hq] = -slope_ref[hq] * dist
        for ref in (klo_s, khi_s, vlo_s, vhi_s):
            ref[1, tm:tm + BLOCK, :] = jnp.zeros((BLOCK, N_KV_HEADS * slab), BF16)

    def split_heads(t, lo_ref, hi_ref, cur):
        zero = jnp.zeros((tm, slab), F32)
        for p in range(N_KV_HEADS // 2):
            pair = t[:, p * slab:(p + 1) * slab]
            swapped = pltpu.roll(pair, HEAD_DIM, axis=1)
            even, odd = 2 * p, 2 * p + 1
            lo_ref[cur, BLOCK:BLOCK + tm, even * slab:(even + 1) * slab] = jnp.where(low_lanes, pair, zero).astype(BF16)
            hi_ref[cur, BLOCK:BLOCK + tm, even * slab:(even + 1) * slab] = jnp.where(low_lanes, zero, swapped).astype(BF16)
            lo_ref[cur, BLOCK:BLOCK + tm, odd * slab:(odd + 1) * slab] = jnp.where(low_lanes, swapped, zero).astype(BF16)
            hi_ref[cur, BLOCK:BLOCK + tm, odd * slab:(odd + 1) * slab] = jnp.where(low_lanes, zero, pair).astype(BF16)

    def project(cur):
        prev = 1 - cur
        xb = x_ref[...]
        scale = HEAD_DIM ** -0.5
        q0 = (jnp.dot(xb, wq0_ref[...], preferred_element_type=F32) + bq0_ref[...]) * scale
        q_s[cur, :, :half_w] = q0.astype(BF16)
        q1 = (jnp.dot(xb, wq1_ref[...], preferred_element_type=F32) + bq1_ref[...]) * scale
        q_s[cur, :, half_w:] = q1.astype(BF16)
        split_heads(jnp.dot(xb, wk_ref[...], preferred_element_type=F32) + bk_ref[...], klo_s, khi_s, cur)
        split_heads(jnp.dot(xb, wv_ref[...], preferred_element_type=F32) + bv_ref[...], vlo_s, vhi_s, cur)
        for ref in (klo_s, khi_s, vlo_s, vhi_s):
            ref[cur, 0:BLOCK, :] = ref[prev, tm:tm + BLOCK, :]

    def attend(prev):
        for jb in range(tm // BLOCK):
            r0 = jb * BLOCK
            if jb == 0:
                mask = in_window & (si >= jnp.where(i == 1, BLOCK, 0))
            else:
                mask = in_window
            for h in range(N_KV_HEADS):
                cols = slice(h * slab, (h + 1) * slab)
                band = slice(r0, r0 + 2 * BLOCK)
                kcat = jnp.concatenate([klo_s[prev, band, cols], khi_s[prev, band, cols]], axis=0)
                vcat = jnp.concatenate([vlo_s[prev, band, cols], vhi_s[prev, band, cols]], axis=0)
                first_slab = h * q_slabs_per_kv
                qst = jnp.concatenate(
                    [q_s[prev, r0:r0 + BLOCK, (first_slab + s) * slab:(first_slab + s + 1) * slab]
                     for s in range(q_slabs_per_kv)], axis=0)
                scores = lax.dot_general(qst, kcat, (((1,), (1,)), ((), ())), preferred_element_type=F32)
                e_rows, den_rows = [], []
                for s in range(q_slabs_per_kv):
                    e_cols, dens = [], []
                    for odd in range(2):
                        hq = 2 * (first_slab + s) + odd
                        blk = scores[s * BLOCK:(s + 1) * BLOCK, odd * 2 * BLOCK:(odd + 1) * 2 * BLOCK]
                        sc = jnp.where(mask, blk + bias_s[hq], -jnp.inf)
                        sink = sink_ref[hq]
                        m = jnp.maximum(jnp.max(sc, axis=-1, keepdims=True), sink)
                        e = jnp.exp(sc - m)
                        dens.append(jnp.sum(e, axis=-1, keepdims=True) + jnp.exp(sink - m))
                        e_cols.append(e.astype(BF16))
                    e_rows.append(jnp.concatenate(e_cols, axis=1))
                    den_rows.append(jnp.where(low_lanes_block, dens[0], dens[1]))
                pv = jnp.dot(jnp.concatenate(e_rows, axis=0), vcat, preferred_element_type=F32)
                out = pv / jnp.concatenate(den_rows, axis=0)
                for s in range(q_slabs_per_kv):
                    o_ref[r0:r0 + BLOCK, (first_slab + s) * slab:(first_slab + s + 1) * slab] = (
                        out[s * BLOCK:(s + 1) * BLOCK, :].astype(o_ref.dtype))

    _pipelined_steps(i, n_tiles, project, attend)


def _attn_call(l, xb, w_in, b_in, sinks, slopes, *, tm=256):
    s, d = xb.shape
    half = ATTN_WIDTH // 2
    n_tiles = s // tm
    assert GQA_GROUP % 2 == 0 and N_KV_HEADS % 2 == 0
    wq0, bq0 = _in_cols(l, d, COL_Q, half, resident=True)
    wq1, bq1 = _in_cols(l, d, COL_Q + half, half, resident=True)
    wk, bk = _in_cols(l, d, COL_K, KV_WIDTH, resident=True)
    wv, bv = _in_cols(l, d, COL_V, KV_WIDTH, resident=True)
    kv_scratch = pltpu.VMEM((2, tm + BLOCK, N_KV_HEADS * 2 * HEAD_DIM), BF16)
    return pl.pallas_call(
        functools.partial(_attn_kernel, tm=tm, n_tiles=n_tiles),
        out_shape=jax.ShapeDtypeStruct((s, ATTN_WIDTH), BF16),
        grid=(n_tiles + 1,),
        in_specs=[
            pl.BlockSpec((tm, d), lambda i: (jnp.minimum(i, n_tiles - 1), 0)),
            wq0, bq0, wq1, bq1, wk, bk, wv, bv,
            pl.BlockSpec(memory_space=pltpu.SMEM),
            pl.BlockSpec(memory_space=pltpu.SMEM),
        ],
        out_specs=pl.BlockSpec((tm, ATTN_WIDTH), lambda i: (jnp.maximum(i - 1, 0), 0)),
        scratch_shapes=[
            pltpu.VMEM((2, tm, ATTN_WIDTH), BF16),
            kv_scratch, kv_scratch, kv_scratch, kv_scratch,
            pltpu.VMEM((N_Q_HEADS, BLOCK, 2 * BLOCK), F32),
        ],
        compiler_params=_params(1),
        name="attn",
    )(xb, w_in, b_in, w_in, b_in, w_in, b_in, w_in, b_in, sinks, slopes)


def _merge_kernel(x_ref, wga_ref, wgb_ref, wgc_ref, bga_ref, bgb_ref, bgc_ref,
                  ya_ref, yb_ref, yc_ref, wba_ref, wbb_ref, wbc_ref, o_ref, wg_s, wb_s):
    row_starts = (0, CONF_WIDTH, CONF_WIDTH + SC_WIDTH, MIX_WIDTH)

    @pl.when(pl.program_id(1) == 0)
    def _():
        for gi, wg_ref in enumerate((wga_ref, wgb_ref, wgc_ref)):
            wg_s[gi] = wg_ref[...].astype(BF16)
        for bi, wb_ref in enumerate((wba_ref, wbb_ref, wbc_ref)):
            wb_s[row_starts[bi]:row_starts[bi + 1], :] = wb_ref[...].astype(BF16)

    xb = x_ref[...]

    def gated(gi, bg_ref, y_ref):
        gate = jax.nn.sigmoid(jnp.dot(xb, wg_s[gi], preferred_element_type=F32) + bg_ref[...])
        wb = wb_s[row_starts[gi]:row_starts[gi + 1], :]
        return gate * jnp.dot(y_ref[...], wb, preferred_element_type=F32)

    merged = gated(0, bga_ref, ya_ref)
    merged = merged + gated(1, bgb_ref, yb_ref)
    merged = merged + gated(2, bgc_ref, yc_ref)
    o_ref[...] = merged.astype(o_ref.dtype)


def _merge_call(l, xb, w_in, b_in, ya, yb, yc, w_branch, *, tm=512, tn=256):
    s, d = xb.shape
    nj = d // tn
    assert COL_GATE % tn == 0
    gate0 = COL_GATE // tn
    assert SC_WIDTH == CONF_WIDTH and ATTN_WIDTH == CONF_WIDTH + SC_WIDTH

    def wg_spec(gi):
        return pl.BlockSpec((None, d, tn), lambda j, i: (l, 0, gate0 + gi * nj + j))

    def bg_spec(gi):
        return pl.BlockSpec((None, 1, tn), lambda j, i: (l, 0, gate0 + gi * nj + j))

    return pl.pallas_call(
        _merge_kernel,
        out_shape=jax.ShapeDtypeStruct((s, d), BF16),
        grid=(nj, s // tm),
        in_specs=[
            pl.BlockSpec((tm, d), lambda j, i: (i, 0)),
            wg_spec(0), wg_spec(1), wg_spec(2),
            bg_spec(0), bg_spec(1), bg_spec(2),
            pl.BlockSpec((tm, CONF_WIDTH), lambda j, i: (i, 0)),
            pl.BlockSpec((tm, SC_WIDTH), lambda j, i: (i, 0)),
            pl.BlockSpec((tm, ATTN_WIDTH), lambda j, i: (i, 0)),
            pl.BlockSpec((None, CONF_WIDTH, tn), lambda j, i: (l, 0, j)),
            pl.BlockSpec((None, SC_WIDTH, tn), lambda j, i: (l, 1, j)),
            pl.BlockSpec((None, ATTN_WIDTH, tn), lambda j, i: (l, 1, j)),
        ],
        out_specs=pl.BlockSpec((tm, tn), lambda j, i: (i, j)),
        scratch_shapes=[pltpu.VMEM((N_GATES, d, tn), BF16), pltpu.VMEM((MIX_WIDTH, tn), BF16)],
        compiler_params=_params(2),
        name="merge",
    )(xb, w_in, w_in, w_in, b_in, b_in, b_in, ya, yb, yc, w_branch, w_branch, w_branch)


def _out_ln_kernel(a_ref, w_ref, b_ref, res_ref, g_ref, beta_ref, of_ref, ob_ref, z_s, *, nn, n_tiles, alpha):
    i = pl.program_id(0)
    j = pl.program_id(1)
    slab_rows = of_ref.shape[0]

    def matmul_step(par):
        z = jnp.dot(a_ref[...], w_ref[j], preferred_element_type=F32) + b_ref[...]
        z_s[par, j] = z + alpha * res_ref[...]

    def ln_step(par):
        def load_z(rows):
            return jnp.concatenate([z_s[par, jj, rows, :] for jj in range(nn)], axis=-1)

        _ln_row_slab(load_z, pl.multiple_of(j * slab_rows, slab_rows), g_ref, beta_ref, of_ref, ob_ref)

    _pipelined_steps(i, n_tiles, matmul_step, ln_step)


def _out_ln_call(l, a, w_slabs, b, res, g, beta, *, alpha, tm=256):
    s, kdim = a.shape
    nn, tn = w_slabs.shape[1], w_slabs.shape[3]
    d = nn * tn
    n_tiles = s // tm
    slab_rows = tm // nn
    assert slab_rows % (2 * SUBLANES) == 0

    def tile(i):
        return jnp.minimum(i, n_tiles - 1)

    def col(i, j):
        return jnp.where(i == n_tiles, nn - 1, j)

    def out_slab(i, j):
        return (jnp.where(i == 0, 0, (i - 1) * nn + j), 0)

    return pl.pallas_call(
        functools.partial(_out_ln_kernel, nn=nn, n_tiles=n_tiles, alpha=alpha),
        out_shape=(jax.ShapeDtypeStruct((s, d), F32), jax.ShapeDtypeStruct((s, d), BF16)),
        grid=(n_tiles + 1, nn),
        in_specs=[
            pl.BlockSpec((tm, kdim), lambda i, j: (tile(i), 0)),
            _resident((None, nn, kdim, tn), lambda i, j: (l, 0, 0, 0)),
            pl.BlockSpec((None, 1, tn), lambda i, j: (l, 0, col(i, j))),
            pl.BlockSpec((tm, tn), lambda i, j: (tile(i), col(i, j))),
            pl.BlockSpec((None, 1, d), lambda i, j: (l, 0, 0)),
            pl.BlockSpec((None, 1, d), lambda i, j: (l, 0, 0)),
        ],
        out_specs=(pl.BlockSpec((slab_rows, d), out_slab), pl.BlockSpec((slab_rows, d), out_slab)),
        scratch_shapes=[pltpu.VMEM((2, nn, tm, tn), F32)],
        compiler_params=_params(2),
        name="out_ln",
    )(a, w_slabs, b, res, g, beta)


def _mlp2_ln_kernel(a_ref, w_ref, b_ref, res_ref, g_ref, beta_ref, of_ref, ob_ref, z_s, *, nk, n_tiles, alpha):
    i = pl.program_id(0)
    kk = pl.program_id(1)
    d = z_s.shape[-1]
    slab_rows = of_ref.shape[0]
    slab = pl.ds(pl.multiple_of(kk * slab_rows, slab_rows), slab_rows)
    slab_w = min(MM_LN_SLAB, d)

    def matmul_step(par):
        for n0 in range(0, d, slab_w):
            part = jnp.dot(a_ref[...], w_ref[:, n0:n0 + slab_w].astype(BF16), preferred_element_type=F32)
            z_s[par, :, n0:n0 + slab_w] = jnp.where(kk == 0, part, z_s[par, :, n0:n0 + slab_w] + part)
        z_s[par, slab, :] += alpha * res_ref[...] + b_ref[...]

    def ln_step(par):
        _ln_row_slab(lambda rows: z_s[par, rows, :], pl.multiple_of(kk * slab_rows, slab_rows),
                     g_ref, beta_ref, of_ref, ob_ref)

    @pl.when((i == 0) & (kk == 0))
    def _():
        z_s[...] = jnp.zeros(z_s.shape, F32)

    _pipelined_steps(i, n_tiles, matmul_step, ln_step)


def _mlp2_ln_call(l, a, w, b, res, g, beta, *, alpha, tm=1024, tk=512):
    s, kdim = a.shape
    d = w.shape[-1]
    nk = kdim // tk
    n_tiles = s // tm
    slab_rows = tm // nk
    assert tm % nk == 0 and slab_rows % (2 * SUBLANES) == 0

    def tile(i):
        return jnp.minimum(i, n_tiles - 1)

    def kstep(i, k):
        return jnp.where(i == n_tiles, nk - 1, k)

    def out_slab(i, k):
        return (jnp.where(i == 0, 0, (i - 1) * nk + k), 0)

    return pl.pallas_call(
        functools.partial(_mlp2_ln_kernel, nk=nk, n_tiles=n_tiles, alpha=alpha),
        out_shape=(jax.ShapeDtypeStruct((s, d), F32), jax.ShapeDtypeStruct((s, d), BF16)),
        grid=(n_tiles + 1, nk),
        in_specs=[
            pl.BlockSpec((tm, tk), lambda i, k: (tile(i), kstep(i, k))),
            pl.BlockSpec((None, tk, d), lambda i, k: (l, kstep(i, k), 0)),
            pl.BlockSpec((None, 1, d), lambda i, k: (l, 0, 0)),
            pl.BlockSpec((slab_rows, d), lambda i, k: (tile(i) * nk + kstep(i, k), 0)),
            pl.BlockSpec((None, 1, d), lambda i, k: (l, 0, 0)),
            pl.BlockSpec((None, 1, d), lambda i, k: (l, 0, 0)),
        ],
        out_specs=(pl.BlockSpec((slab_rows, d), out_slab), pl.BlockSpec((slab_rows, d), out_slab)),
        scratch_shapes=[pltpu.VMEM((2, tm, d), F32)],
        compiler_params=_params(2),
        name="mlp2_ln",
    )(a, w, b, res, g, beta)


def _mlp1_kernel(x_ref, w_ref, b_ref, o_ref):
    h = jnp.dot(x_ref[...], w_ref[...].astype(BF16), preferred_element_type=F32) + b_ref[...]
    h = jnp.maximum(h, 0.0)
    o_ref[...] = (h * h).astype(o_ref.dtype)


def _mlp1_call(l, xb, w, b, *, tm=1024, tn=512):
    s, d = xb.shape
    f = w.shape[-1]
    return pl.pallas_call(
        _mlp1_kernel,
        out_shape=jax.ShapeDtypeStruct((s, f), BF16),
        grid=(s // tm, f // tn),
        in_specs=[
            pl.BlockSpec((tm, d), lambda i, j: (i, 0)),
            pl.BlockSpec((None, d, tn), lambda i, j: (l, 0, j)),
            pl.BlockSpec((None, 1, tn), lambda i, j: (l, 0, j)),
        ],
        out_specs=pl.BlockSpec((tm, tn), lambda i, j: (i, j)),
        compiler_params=_params(2),
        name="mlp1",
    )(xb, w, b)


def kernel(x, w_in, b_in, conf_dw_w, conf_dw_b, conf_ln_g, conf_ln_b, sc_dw_w, attn_sinks, w_branch, w_out, b_out,
           ln1_g, ln1_b, w_mlp1, b_mlp1, w_mlp2, b_mlp2, ln2_g, ln2_b):
    batch, seq, d_model = x.shape
    depth = w_in.shape[0]
    alpha = float((2 * depth) ** 0.25)
    assert batch == 1
    assert w_in.shape[-1] == COL_GATE + N_GATES * d_model

    def row(v):
        return v[:, None, :]

    w_in_b = w_in[:, :, :COL_GATE].astype(BF16)
    out_slab = min(OUT_LN_SLAB, d_model)
    w_out_b = w_out.astype(BF16).reshape(depth, d_model, d_model // out_slab, out_slab).transpose(0, 2, 1, 3)
    b_in_r = row(b_in)
    slopes = jnp.asarray(_alibi_slopes(N_Q_HEADS))

    xf = x.reshape(seq, d_model)
    xb = xf.astype(BF16)
    for l in range(depth):
        ya = _conf_call(l, xb, w_in_b, b_in_r, conf_dw_w, row(conf_dw_b), row(conf_ln_g), row(conf_ln_b))
        yb = _sconv_call(l, xb, w_in_b, b_in_r, sc_dw_w)
        yc = _attn_call(l, xb, w_in_b, b_in_r, attn_sinks[l], slopes)
        merged = _merge_call(l, xb, w_in, b_in_r, ya, yb, yc, w_branch)
        xf, xb = _out_ln_call(l, merged, w_out_b, row(b_out), xf, row(ln1_g), row(ln1_b), alpha=alpha)
        hid = _mlp1_call(l, xb, w_mlp1, row(b_mlp1))
        xf, xb = _mlp2_ln_call(l, hid, w_mlp2, row(b_mlp2), xf, row(ln2_g), row(ln2_b), alpha=alpha)
    return xf.reshape(batch, seq, d_model)
```

```python
import functools

import jax
import jax.numpy as jnp
import numpy as np
from jax import lax
from jax.experimental import pallas as pl
from jax.experimental.pallas import tpu as pltpu

F32 = jnp.float32
BF16 = jnp.bfloat16

CONF_WIDTH = 768
CONF_KERNEL = 31
SC_WIDTH = 768
SC_KERNEL = 3
HEAD_DIM = 64
N_Q_HEADS = 24
N_KV_HEADS = 4
GQA_GROUP = N_Q_HEADS // N_KV_HEADS
ATTN_WIDTH = N_Q_HEADS * HEAD_DIM
KV_WIDTH = N_KV_HEADS * HEAD_DIM
MIX_WIDTH = CONF_WIDTH + SC_WIDTH + ATTN_WIDTH
WINDOW = 128
BLOCK = 128
N_GATES = 3
LN_EPS = 1e-5

COL_CONF = 0
COL_SC = COL_CONF + 2 * CONF_WIDTH
COL_Q = COL_SC + 3 * SC_WIDTH
COL_K = COL_Q + ATTN_WIDTH
COL_V = COL_K + KV_WIDTH
COL_GATE = COL_V + KV_WIDTH

SUBLANES = 8

V7X_VMEM_BYTES = 64 * 1024 * 1024
VMEM_LIMIT_BYTES = V7X_VMEM_BYTES - 4 * 1024 * 1024

CONF_HIST = 32
CONF_CHUNK = 32
CONF_COLS = 256
SC_HIST = 8
SC_CHUNK = 64
LN_CHUNK = 8
MM_LN_SLAB = 1024
OUT_LN_SLAB = 2048


def _alibi_slopes(n):
    def pow2(m):
        start = 2.0 ** (-8.0 / m)
        return [start ** (i + 1) for i in range(m)]
    if (n & (n - 1)) == 0:
        s = pow2(n)
    else:
        c = 2 ** int(np.floor(np.log2(n)))
        s = pow2(c) + pow2(2 * c)[0::2][: n - c]
    return np.array(s, dtype=np.float32)


def _params(n_axes):
    return pltpu.CompilerParams(
        dimension_semantics=("arbitrary",) * n_axes,
        vmem_limit_bytes=VMEM_LIMIT_BYTES,
    )


def _resident(block_shape, index_map):
    return pl.BlockSpec(block_shape, index_map, pipeline_mode=pl.Buffered(1))


def _in_cols(l, d, col, width, *, resident):
    assert col % width == 0
    make = _resident if resident else pl.BlockSpec
    return (make((None, d, width), lambda *_: (l, 0, col // width)),
            make((None, 1, width), lambda *_: (l, 0, col // width)))


def _layer_norm_rows(z, g, b):
    mu = jnp.mean(z, axis=-1, keepdims=True)
    zc = z - mu
    var = jnp.mean(zc * zc, axis=-1, keepdims=True)
    return zc * lax.rsqrt(var + LN_EPS) * g + b


def _ln_row_slab(load_z, first_row, g_ref, beta_ref, of_ref, ob_ref):
    for c0 in range(0, of_ref.shape[0], LN_CHUNK):
        y = _layer_norm_rows(load_z(pl.ds(first_row + c0, LN_CHUNK)), g_ref[...], beta_ref[...])
        of_ref[c0:c0 + LN_CHUNK, :] = y
        ob_ref[c0:c0 + LN_CHUNK, :] = y.astype(ob_ref.dtype)


def _pipelined_steps(i, n_tiles, produce, consume):
    @pl.when(i == 0)
    def _():
        produce(0)

    for par in (0, 1):
        @pl.when((i > 0) & (i < n_tiles) & (i % 2 == par))
        def _():
            consume(1 - par)
            produce(par)

    @pl.when(i == n_tiles)
    def _():
        consume((n_tiles - 1) % 2)


def _conf_kernel(x_ref, w_ref, b_ref, dww_ref, dwb_ref, g_ref, beta_ref, o_ref, ubuf0, ubuf1, wtap, *, tm, n_tiles):
    c = CONF_WIDTH
    i = pl.program_id(0)
    ubufs = (ubuf0, ubuf1)

    @pl.when(i == 0)
    def _():
        for ubuf in ubufs:
            ubuf[...] = jnp.zeros(ubuf.shape, F32)
        for k in range(CONF_KERNEL):
            wtap[k] = jnp.broadcast_to(dww_ref[k:k + 1, :], (SUBLANES, c))

    def project(par):
        cur, prev = ubufs[par], ubufs[1 - par]
        xb = x_ref[...]
        for c0 in range(0, c, CONF_COLS):
            cols = slice(c0, c0 + CONF_COLS)
            gcols = slice(c + c0, c + c0 + CONF_COLS)
            pa = jnp.dot(xb, w_ref[:, cols], preferred_element_type=F32) + b_ref[:, cols]
            pg = jnp.dot(xb, w_ref[:, gcols], preferred_element_type=F32) + b_ref[:, gcols]
            u = pa * jax.nn.sigmoid(pg)
            for s in range(SUBLANES):
                cur[s, 0:CONF_HIST, cols] = prev[s, tm:tm + CONF_HIST, cols]
                cur[s, CONF_HIST - s:CONF_HIST - s + tm, cols] = u

    first_tap = CONF_HIST - (CONF_KERNEL - 1)

    def convolve(par):
        ubuf = ubufs[par]
        for base in range(0, tm, CONF_CHUNK):
            acc = jnp.broadcast_to(dwb_ref[...], (CONF_CHUNK, c))
            for k in range(CONF_KERNEL):
                s = (first_tap + k) % SUBLANES
                off = first_tap + k - s
                wk = jnp.concatenate([wtap[k]] * (CONF_CHUNK // SUBLANES), axis=0)
                acc = acc + wk * ubuf[s, base + off:base + off + CONF_CHUNK, :]
            y = _layer_norm_rows(acc, g_ref[...], beta_ref[...])
            o_ref[base:base + CONF_CHUNK, :] = (y * jax.nn.sigmoid(y)).astype(o_ref.dtype)

    _pipelined_steps(i, n_tiles, project, convolve)


def _conf_call(l, xb, w_in, b_in, dww, dwb, g, beta, *, tm=256):
    s, d = xb.shape
    c = CONF_WIDTH
    n_tiles = s // tm
    return pl.pallas_call(
        functools.partial(_conf_kernel, tm=tm, n_tiles=n_tiles),
        out_shape=jax.ShapeDtypeStruct((s, c), BF16),
        grid=(n_tiles + 1,),
        in_specs=[
            pl.BlockSpec((tm, d), lambda i: (jnp.minimum(i, n_tiles - 1), 0)),
            *_in_cols(l, d, COL_CONF, 2 * c, resident=True),
            _resident((None, CONF_KERNEL, c), lambda i: (l, 0, 0)),
            _resident((None, 1, c), lambda i: (l, 0, 0)),
            _resident((None, 1, c), lambda i: (l, 0, 0)),
            _resident((None, 1, c), lambda i: (l, 0, 0)),
        ],
        out_specs=pl.BlockSpec((tm, c), lambda i: (jnp.maximum(i - 1, 0), 0)),
        scratch_shapes=[pltpu.VMEM((SUBLANES, tm + CONF_HIST, c), F32),
                        pltpu.VMEM((SUBLANES, tm + CONF_HIST, c), F32),
                        pltpu.VMEM((CONF_KERNEL, SUBLANES, c), F32)],
        compiler_params=_params(1),
        name="conf",
    )(xb, w_in, b_in, dww, dwb, g, beta)


def _sconv_kernel(x_ref, wb_ref, bb_ref, wc_ref, bc_ref, wx_ref, bx_ref, dww_ref, o_ref, vbuf, pbuf, *, tm):
    c = SC_WIDTH

    @pl.when(pl.program_id(0) == 0)
    def _():
        vbuf[0:SC_HIST, :] = jnp.zeros((SC_HIST, c), F32)

    xb = x_ref[...]
    pbuf[...] = jnp.dot(xb, wb_ref[...], preferred_element_type=F32) + bb_ref[...]
    pc = jnp.dot(xb, wc_ref[...], preferred_element_type=F32) + bc_ref[...]
    px = jnp.dot(xb, wx_ref[...], preferred_element_type=F32) + bx_ref[...]
    vbuf[SC_HIST:SC_HIST + tm, :] = pc * px

    first_tap = SC_HIST - (SC_KERNEL - 1)

    def chunk(ci, carry):
        base = pl.multiple_of(ci * SC_CHUNK, SC_CHUNK)
        win = vbuf[pl.ds(base, SC_CHUNK + SC_HIST), :]
        acc = dww_ref[0:1, :] * win[first_tap:first_tap + SC_CHUNK, :]
        for k in range(1, SC_KERNEL):
            acc = acc + dww_ref[k:k + 1, :] * win[first_tap + k:first_tap + k + SC_CHUNK, :]
        o_ref[pl.ds(base, SC_CHUNK), :] = (pbuf[pl.ds(base, SC_CHUNK), :] * acc).astype(o_ref.dtype)
        return carry

    lax.fori_loop(0, tm // SC_CHUNK, chunk, 0)
    vbuf[0:SC_HIST, :] = vbuf[tm:tm + SC_HIST, :]


def _sconv_call(l, xb, w_in, b_in, dww, *, tm=512):
    s, d = xb.shape
    c = SC_WIDTH
    wb, bb = _in_cols(l, d, COL_SC, c, resident=True)
    wc, bc = _in_cols(l, d, COL_SC + c, c, resident=True)
    wx, bx = _in_cols(l, d, COL_SC + 2 * c, c, resident=True)
    return pl.pallas_call(
        functools.partial(_sconv_kernel, tm=tm),
        out_shape=jax.ShapeDtypeStruct((s, c), BF16),
        grid=(s // tm,),
        in_specs=[
            pl.BlockSpec((tm, d), lambda i: (i, 0)),
            wb, bb, wc, bc, wx, bx,
            _resident((None, SC_KERNEL, c), lambda i: (l, 0, 0)),
        ],
        out_specs=pl.BlockSpec((tm, c), lambda i: (i, 0)),
        scratch_shapes=[pltpu.VMEM((tm + SC_HIST, c), F32), pltpu.VMEM((tm, c), F32)],
        compiler_params=_params(1),
        name="sconv",
    )(xb, w_in, b_in, w_in, b_in, w_in, b_in, dww)


def _attn_kernel(x_ref, wq0_ref, bq0_ref, wq1_ref, bq1_ref, wk_ref, bk_ref, wv_ref, bv_ref, sink_ref, slope_ref,
                 o_ref, q_s, klo_s, khi_s, vlo_s, vhi_s, bias_s, *, tm, n_tiles):
    i = pl.program_id(0)
    half_w = ATTN_WIDTH // 2
    slab = 2 * HEAD_DIM
    q_slabs_per_kv = GQA_GROUP // 2
    low_lanes = lax.broadcasted_iota(jnp.int32, (tm, slab), 1) < HEAD_DIM
    low_lanes_block = lax.broadcasted_iota(jnp.int32, (BLOCK, slab), 1) < HEAD_DIM

    qi = lax.broadcasted_iota(jnp.int32, (BLOCK, 2 * BLOCK), 0)
    si = lax.broadcasted_iota(jnp.int32, (BLOCK, 2 * BLOCK), 1)
    dist_i = BLOCK + qi - si
    in_window = (dist_i >= 0) & (dist_i < WINDOW)

    @pl.when(i == 0)
    def _():
        dist = dist_i.astype(F32)
        for hq in range(N_Q_HEADS):
            bias_s[hq] = -slope_ref[hq] * dist
        for ref in (klo_s, khi_s, vlo_s, vhi_s):
            ref[1, tm:tm + BLOCK, :] = jnp.zeros((BLOCK, N_KV_HEADS * slab), BF16)

    def split_heads(t, lo_ref, hi_ref, cur):
        zero = jnp.zeros((tm, slab), F32)
        for p in range(N_KV_HEADS // 2):
            pair = t[:, p * slab:(p + 1) * slab]
            swapped = pltpu.roll(pair, HEAD_DIM, axis=1)
            even, odd = 2 * p, 2 * p + 1
            lo_ref[cur, BLOCK:BLOCK + tm, even * slab:(even + 1) * slab] = jnp.where(low_lanes, pair, zero).astype(BF16)
            hi_ref[cur, BLOCK:BLOCK + tm, even * slab:(even + 1) * slab] = jnp.where(low_lanes, zero, swapped).astype(BF16)
            lo_ref[cur, BLOCK:BLOCK + tm, odd * slab:(odd + 1) * slab] = jnp.where(low_lanes, swapped, zero).astype(BF16)
            hi_ref[cur, BLOCK:BLOCK + tm, odd * slab:(odd + 1) * slab] = jnp.where(low_lanes, zero, pair).astype(BF16)

    def project(cur):
        prev = 1 - cur
        xb = x_ref[...]
        scale = HEAD_DIM ** -0.5
        q0 = (jnp.dot(xb, wq0_ref[...], preferred_element_type=F32) + bq0_ref[...]) * scale
        q_s[cur, :, :half_w] = q0.astype(BF16)
        q1 = (jnp.dot(xb, wq1_ref[...], preferred_element_type=F32) + bq1_ref[...]) * scale
        q_s[cur, :, half_w:] = q1.astype(BF16)
        split_heads(jnp.dot(xb, wk_ref[...], preferred_element_type=F32) + bk_ref[...], klo_s, khi_s, cur)
        split_heads(jnp.dot(xb, wv_ref[...], preferred_element_type=F32) + bv_ref[...], vlo_s, vhi_s, cur)
        for ref in (klo_s, khi_s, vlo_s, vhi_s):
            ref[cur, 0:BLOCK, :] = ref[prev, tm:tm + BLOCK, :]

    def attend(prev):
        for jb in range(tm // BLOCK):
            r0 = jb * BLOCK
            if jb == 0:
                mask = in_window & (si >= jnp.where(i == 1, BLOCK, 0))
            else:
                mask = in_window
            for h in range(N_KV_HEADS):
                cols = slice(h * slab, (h + 1) * slab)
                band = slice(r0, r0 + 2 * BLOCK)
                kcat = jnp.concatenate([klo_s[prev, band, cols], khi_s[prev, band, cols]], axis=0)
                vcat = jnp.concatenate([vlo_s[prev, band, cols], vhi_s[prev, band, cols]], axis=0)
                first_slab = h * q_slabs_per_kv
                qst = jnp.concatenate(
                    [q_s[prev, r0:r0 + BLOCK, (first_slab + s) * slab:(first_slab + s + 1) * slab]
                     for s in range(q_slabs_per_kv)], axis=0)
                scores = lax.dot_general(qst, kcat, (((1,), (1,)), ((), ())), preferred_element_type=F32)
                e_rows, den_rows = [], []
                for s in range(q_slabs_per_kv):
                    e_cols, dens = [], []
                    for odd in range(2):
                        hq = 2 * (first_slab + s) + odd
                        blk = scores[s * BLOCK:(s + 1) * BLOCK, odd * 2 * BLOCK:(odd + 1) * 2 * BLOCK]
                        sc = jnp.where(mask, blk + bias_s[hq], -jnp.inf)
                        sink = sink_ref[hq]
                        m = jnp.maximum(jnp.max(sc, axis=-1, keepdims=True), sink)
                        e = jnp.exp(sc - m)
                        dens.append(jnp.sum(e, axis=-1, keepdims=True) + jnp.exp(sink - m))
                        e_cols.append(e.astype(BF16))
                    e_rows.append(jnp.concatenate(e_cols, axis=1))
                    den_rows.append(jnp.where(low_lanes_block, dens[0], dens[1]))
                pv = jnp.dot(jnp.concatenate(e_rows, axis=0), vcat, preferred_element_type=F32)
                out = pv / jnp.concatenate(den_rows, axis=0)
                for s in range(q_slabs_per_kv):
                    o_ref[r0:r0 + BLOCK, (first_slab + s) * slab:(first_slab + s + 1) * slab] = (
                        out[s * BLOCK:(s + 1) * BLOCK, :].astype(o_ref.dtype))

    _pipelined_steps(i, n_tiles, project, attend)


def _attn_call(l, xb, w_in, b_in, sinks, slopes, *, tm=256):
    s, d = xb.shape
    half = ATTN_WIDTH // 2
    n_tiles = s // tm
    assert GQA_GROUP % 2 == 0 and N_KV_HEADS % 2 == 0
    wq0, bq0 = _in_cols(l, d, COL_Q, half, resident=True)
    wq1, bq1 = _in_cols(l, d, COL_Q + half, half, resident=True)
    wk, bk = _in_cols(l, d, COL_K, KV_WIDTH, resident=True)
    wv, bv = _in_cols(l, d, COL_V, KV_WIDTH, resident=True)
    kv_scratch = pltpu.VMEM((2, tm + BLOCK, N_KV_HEADS * 2 * HEAD_DIM), BF16)
    return pl.pallas_call(
        functools.partial(_attn_kernel, tm=tm, n_tiles=n_tiles),
        out_shape=jax.ShapeDtypeStruct((s, ATTN_WIDTH), BF16),
        grid=(n_tiles + 1,),
        in_specs=[
            pl.BlockSpec((tm, d), lambda i: (jnp.minimum(i, n_tiles - 1), 0)),
            wq0, bq0, wq1, bq1, wk, bk, wv, bv,
            pl.BlockSpec(memory_space=pltpu.SMEM),
            pl.BlockSpec(memory_space=pltpu.SMEM),
        ],
        out_specs=pl.BlockSpec((tm, ATTN_WIDTH), lambda i: (jnp.maximum(i - 1, 0), 0)),
        scratch_shapes=[
            pltpu.VMEM((2, tm, ATTN_WIDTH), BF16),
            kv_scratch, kv_scratch, kv_scratch, kv_scratch,
            pltpu.VMEM((N_Q_HEADS, BLOCK, 2 * BLOCK), F32),
        ],
        compiler_params=_params(1),
        name="attn",
    )(xb, w_in, b_in, w_in, b_in, w_in, b_in, w_in, b_in, sinks, slopes)


def _merge_kernel(x_ref, wga_ref, wgb_ref, wgc_ref, bga_ref, bgb_ref, bgc_ref,
                  ya_ref, yb_ref, yc_ref, wba_ref, wbb_ref, wbc_ref, o_ref, wg_s, wb_s):
    row_starts = (0, CONF_WIDTH, CONF_WIDTH + SC_WIDTH, MIX_WIDTH)

    @pl.when(pl.program_id(1) == 0)
    def _():
        for gi, wg_ref in enumerate((wga_ref, wgb_ref, wgc_ref)):
            wg_s[gi] = wg_ref[...].astype(BF16)
        for bi, wb_ref in enumerate((wba_ref, wbb_ref, wbc_ref)):
            wb_s[row_starts[bi]:row_starts[bi + 1], :] = wb_ref[...].astype(BF16)

    xb = x_ref[...]

    def gated(gi, bg_ref, y_ref):
        gate = jax.nn.sigmoid(jnp.dot(xb, wg_s[gi], preferred_element_type=F32) + bg_ref[...])
        wb = wb_s[row_starts[gi]:row_starts[gi + 1], :]
        return gate * jnp.dot(y_ref[...], wb, preferred_element_type=F32)

    merged = gated(0, bga_ref, ya_ref)
    merged = merged + gated(1, bgb_ref, yb_ref)
    merged = merged + gated(2, bgc_ref, yc_ref)
    o_ref[...] = merged.astype(o_ref.dtype)


def _merge_call(l, xb, w_in, b_in, ya, yb, yc, w_branch, *, tm=512, tn=256):
    s, d = xb.shape
    nj = d // tn
    assert COL_GATE % tn == 0
    gate0 = COL_GATE // tn
    assert SC_WIDTH == CONF_WIDTH and ATTN_WIDTH == CONF_WIDTH + SC_WIDTH

    n_tiles = s // tm

    def ahead(j, i, lead):
        return jnp.minimum(jnp.where(i >= max(1, n_tiles - lead), j + 1, j), nj - 1)

    def wg_spec(gi):
        return pl.BlockSpec((None, d, tn), lambda j, i: (l, 0, gate0 + gi * nj + ahead(j, i, N_GATES + 1 - gi)))

    def bg_spec(gi):
        return pl.BlockSpec((None, 1, tn), lambda j, i: (l, 0, gate0 + gi * nj + j))

    return pl.pallas_call(
        _merge_kernel,
        out_shape=jax.ShapeDtypeStruct((s, d), BF16),
        grid=(nj, s // tm),
        in_specs=[
            pl.BlockSpec((tm, d), lambda j, i: (i, 0)),
            wg_spec(0), wg_spec(1), wg_spec(2),
            bg_spec(0), bg_spec(1), bg_spec(2),
            pl.BlockSpec((tm, CONF_WIDTH), lambda j, i: (i, 0)),
            pl.BlockSpec((tm, SC_WIDTH), lambda j, i: (i, 0)),
            pl.BlockSpec((tm, ATTN_WIDTH), lambda j, i: (i, 0)),
            pl.BlockSpec((None, CONF_WIDTH, tn), lambda j, i: (l, 0, ahead(j, i, 1))),
            pl.BlockSpec((None, SC_WIDTH, tn), lambda j, i: (l, 1, ahead(j, i, 1))),
            pl.BlockSpec((None, ATTN_WIDTH, tn), lambda j, i: (l, 1, ahead(j, i, 1))),
        ],
        out_specs=pl.BlockSpec((tm, tn), lambda j, i: (i, j)),
        scratch_shapes=[pltpu.VMEM((N_GATES, d, tn), BF16), pltpu.VMEM((MIX_WIDTH, tn), BF16)],
        compiler_params=_params(2),
        name="merge",
    )(xb, w_in, w_in, w_in, b_in, b_in, b_in, ya, yb, yc, w_branch, w_branch, w_branch)


def _out_ln_kernel(a_ref, w_ref, b_ref, res_ref, g_ref, beta_ref, of_ref, ob_ref, z_s, *, nn, n_tiles, alpha):
    i = pl.program_id(0)
    j = pl.program_id(1)
    slab_rows = of_ref.shape[0]

    def matmul_step(par):
        z = jnp.dot(a_ref[...], w_ref[j], preferred_element_type=F32) + b_ref[...]
        z_s[par, j] = z + alpha * res_ref[...]

    def ln_step(par):
        def load_z(rows):
            return jnp.concatenate([z_s[par, jj, rows, :] for jj in range(nn)], axis=-1)

        _ln_row_slab(load_z, pl.multiple_of(j * slab_rows, slab_rows), g_ref, beta_ref, of_ref, ob_ref)

    _pipelined_steps(i, n_tiles, matmul_step, ln_step)


def _out_ln_call(l, a, w_slabs, b, res, g, beta, *, alpha, tm=256):
    s, kdim = a.shape
    nn, tn = w_slabs.shape[1], w_slabs.shape[3]
    d = nn * tn
    n_tiles = s // tm
    slab_rows = tm // nn
    assert slab_rows % (2 * SUBLANES) == 0

    def tile(i):
        return jnp.minimum(i, n_tiles - 1)

    def col(i, j):
        return jnp.where(i == n_tiles, nn - 1, j)

    def out_slab(i, j):
        return (jnp.where(i == 0, 0, (i - 1) * nn + j), 0)

    return pl.pallas_call(
        functools.partial(_out_ln_kernel, nn=nn, n_tiles=n_tiles, alpha=alpha),
        out_shape=(jax.ShapeDtypeStruct((s, d), F32), jax.ShapeDtypeStruct((s, d), BF16)),
        grid=(n_tiles + 1, nn),
        in_specs=[
            pl.BlockSpec((tm, kdim), lambda i, j: (tile(i), 0)),
            _resident((None, nn, kdim, tn), lambda i, j: (l, 0, 0, 0)),
            pl.BlockSpec((None, 1, tn), lambda i, j: (l, 0, col(i, j))),
            pl.BlockSpec((tm, tn), lambda i, j: (tile(i), col(i, j))),
            pl.BlockSpec((None, 1, d), lambda i, j: (l, 0, 0)),
            pl.BlockSpec((None, 1, d), lambda i, j: (l, 0, 0)),
        ],
        out_specs=(pl.BlockSpec((slab_rows, d), out_slab), pl.BlockSpec((slab_rows, d), out_slab)),
        scratch_shapes=[pltpu.VMEM((2, nn, tm, tn), F32)],
        compiler_params=_params(2),
        name="out_ln",
    )(a, w_slabs, b, res, g, beta)


def _mlp2_ln_kernel(a_ref, w_ref, b_ref, res_ref, g_ref, beta_ref, of_ref, ob_ref, z_s, *, nk, n_tiles, alpha):
    i = pl.program_id(0)
    kk = pl.program_id(1)
    d = z_s.shape[-1]
    slab_rows = of_ref.shape[0]
    slab = pl.ds(pl.multiple_of(kk * slab_rows, slab_rows), slab_rows)
    slab_w = min(MM_LN_SLAB, d)

    def matmul_step(par):
        for n0 in range(0, d, slab_w):
            part = jnp.dot(a_ref[...], w_ref[:, n0:n0 + slab_w].astype(BF16), preferred_element_type=F32)
            z_s[par, :, n0:n0 + slab_w] = jnp.where(kk == 0, part, z_s[par, :, n0:n0 + slab_w] + part)
        z_s[par, slab, :] += alpha * res_ref[...] + b_ref[...]

    def ln_step(par):
        _ln_row_slab(lambda rows: z_s[par, rows, :], pl.multiple_of(kk * slab_rows, slab_rows),
                     g_ref, beta_ref, of_ref, ob_ref)

    @pl.when((i == 0) & (kk == 0))
    def _():
        z_s[...] = jnp.zeros(z_s.shape, F32)

    _pipelined_steps(i, n_tiles, matmul_step, ln_step)


def _mlp2_ln_call(l, a, w, b, res, g, beta, *, alpha, tm=1024, tk=512):
    s, kdim = a.shape
    d = w.shape[-1]
    nk = kdim // tk
    n_tiles = s // tm
    slab_rows = tm // nk
    assert tm % nk == 0 and slab_rows % (2 * SUBLANES) == 0

    def tile(i):
        return jnp.minimum(i, n_tiles - 1)

    def kstep(i, k):
        return jnp.where(i == n_tiles, nk - 1, k)

    def out_slab(i, k):
        return (jnp.where(i == 0, 0, (i - 1) * nk + k), 0)

    return pl.pallas_call(
        functools.partial(_mlp2_ln_kernel, nk=nk, n_tiles=n_tiles, alpha=alpha),
        out_shape=(jax.ShapeDtypeStruct((s, d), F32), jax.ShapeDtypeStruct((s, d), BF16)),
        grid=(n_tiles + 1, nk),
        in_specs=[
            pl.BlockSpec((tm, tk), lambda i, k: (tile(i), kstep(i, k))),
            pl.BlockSpec((None, tk, d), lambda i, k: (l, kstep(i, k), 0)),
            pl.BlockSpec((None, 1, d), lambda i, k: (l, 0, 0)),
            pl.BlockSpec((slab_rows, d), lambda i, k: (tile(i) * nk + kstep(i, k), 0)),
            pl.BlockSpec((None, 1, d), lambda i, k: (l, 0, 0)),
            pl.BlockSpec((None, 1, d), lambda i, k: (l, 0, 0)),
        ],
        out_specs=(pl.BlockSpec((slab_rows, d), out_slab), pl.BlockSpec((slab_rows, d), out_slab)),
        scratch_shapes=[pltpu.VMEM((2, tm, d), F32)],
        compiler_params=_params(2),
        name="mlp2_ln",
    )(a, w, b, res, g, beta)


def _mlp1_kernel(x_ref, w_ref, b_ref, o_ref):
    h = jnp.dot(x_ref[...], w_ref[...].astype(BF16), preferred_element_type=F32) + b_ref[...]
    h = jnp.maximum(h, 0.0)
    o_ref[...] = (h * h).astype(o_ref.dtype)


def _mlp1_call(l, xb, w, b, *, tm=1024, tn=1024):
    s, d = xb.shape
    f = w.shape[-1]
    return pl.pallas_call(
        _mlp1_kernel,
        out_shape=jax.ShapeDtypeStruct((s, f), BF16),
        grid=(s // tm, f // tn),
        in_specs=[
            pl.BlockSpec((tm, d), lambda i, j: (i, 0)),
            pl.BlockSpec((None, d, tn), lambda i, j: (l, 0, j)),
            pl.BlockSpec((None, 1, tn), lambda i, j: (l, 0, j)),
        ],
        out_specs=pl.BlockSpec((tm, tn), lambda i, j: (i, j)),
        compiler_params=_params(2),
        name="mlp1",
    )(xb, w, b)


def kernel(x, w_in, b_in, conf_dw_w, conf_dw_b, conf_ln_g, conf_ln_b, sc_dw_w, attn_sinks, w_branch, w_out, b_out,
           ln1_g, ln1_b, w_mlp1, b_mlp1, w_mlp2, b_mlp2, ln2_g, ln2_b):
    batch, seq, d_model = x.shape
    depth = w_in.shape[0]
    alpha = float((2 * depth) ** 0.25)
    assert batch == 1
    assert w_in.shape[-1] == COL_GATE + N_GATES * d_model

    def row(v):
        return v[:, None, :]

    w_in_b = w_in[:, :, :COL_GATE].astype(BF16)
    out_slab = min(OUT_LN_SLAB, d_model)
    w_out_b = w_out.astype(BF16).reshape(depth, d_model, d_model // out_slab, out_slab).transpose(0, 2, 1, 3)
    b_in_r = row(b_in)
    slopes = jnp.asarray(_alibi_slopes(N_Q_HEADS))

    xf = x.reshape(seq, d_model)
    xb = xf.astype(BF16)
    for l in range(depth):
        ya = _conf_call(l, xb, w_in_b, b_in_r, conf_dw_w, row(conf_dw_b), row(conf_ln_g), row(conf_ln_b))
        yb = _sconv_call(l, xb, w_in_b, b_in_r, sc_dw_w)
        yc = _attn_call(l, xb, w_in_b, b_in_r, attn_sinks[l], slopes)
        merged = _merge_call(l, xb, w_in, b_in_r, ya, yb, yc, w_branch)
        xf, xb = _out_ln_call(l, merged, w_out_b, row(b_out), xf, row(ln1_g), row(ln1_b), alpha=alpha)
        hid = _mlp1_call(l, xb, w_mlp1, row(b_mlp1))
        xf, xb = _mlp2_ln_call(l, hid, w_mlp2, row(b_mlp2), xf, row(ln2_g), row(ln2_b), alpha=alpha)
    return xf.reshape(batch, seq, d_model)
```

```python
import functools
from typing import NamedTuple

import jax
import jax.numpy as jnp
import numpy as np
from jax import lax
from jax.experimental import pallas as pl
from jax.experimental.pallas import tpu as pltpu

F32 = jnp.float32
BF16 = jnp.bfloat16

CONF_WIDTH = 768
CONF_KERNEL = 31
SC_WIDTH = 768
SC_KERNEL = 3
HEAD_DIM = 64
N_Q_HEADS = 24
N_KV_HEADS = 4
GQA_GROUP = N_Q_HEADS // N_KV_HEADS
ATTN_WIDTH = N_Q_HEADS * HEAD_DIM
KV_WIDTH = N_KV_HEADS * HEAD_DIM
MIX_WIDTH = CONF_WIDTH + SC_WIDTH + ATTN_WIDTH
WINDOW = 128
BLOCK = 128
N_GATES = 3
LN_EPS = 1e-5

COL_CONF = 0
COL_SC = COL_CONF + 2 * CONF_WIDTH
COL_Q = COL_SC + 3 * SC_WIDTH
COL_K = COL_Q + ATTN_WIDTH
COL_V = COL_K + KV_WIDTH
COL_GATE = COL_V + KV_WIDTH

SUBLANES = 8

V7X_VMEM_BYTES = 64 * 1024 * 1024
VMEM_LIMIT_BYTES = V7X_VMEM_BYTES - 4 * 1024 * 1024


class _Tiles(NamedTuple):
    mixer_rows: int = 256
    sconv_rows: int = 512
    merge_rows: int = 512
    merge_cols: int = 256
    out_ln_rows: int = 256
    out_ln_cols: int = 2048
    mlp1_rows: int = 1024
    mlp1_cols: int = 1024
    mlp2_rows: int = 1024
    mlp2_k: int = 512


TILES = _Tiles()

CONF_HIST = 32
CONF_CHUNK = 32
CONF_COLS = 256
SC_HIST = 8
SC_CHUNK = 64
LN_CHUNK = 8
MM_LN_SLAB = 1024


def _alibi_slopes(n):
    def pow2(m):
        start = 2.0 ** (-8.0 / m)
        return [start ** (i + 1) for i in range(m)]
    if (n & (n - 1)) == 0:
        s = pow2(n)
    else:
        c = 2 ** int(np.floor(np.log2(n)))
        s = pow2(c) + pow2(2 * c)[0::2][: n - c]
    return np.array(s, dtype=np.float32)


def _params(n_axes):
    return pltpu.CompilerParams(
        dimension_semantics=("arbitrary",) * n_axes,
        vmem_limit_bytes=VMEM_LIMIT_BYTES,
    )


def _resident(block_shape, index_map):
    return pl.BlockSpec(block_shape, index_map, pipeline_mode=pl.Buffered(1))


def _in_cols(l, d, col, width, *, resident):
    assert col % width == 0
    make = _resident if resident else pl.BlockSpec
    return (make((None, d, width), lambda *_: (l, 0, col // width)),
            make((None, 1, width), lambda *_: (l, 0, col // width)))


def _layer_norm_rows(z, g, b):
    mu = jnp.mean(z, axis=-1, keepdims=True)
    zc = z - mu
    var = jnp.mean(zc * zc, axis=-1, keepdims=True)
    return zc * lax.rsqrt(var + LN_EPS) * g + b


def _ln_row_slab(load_z, first_row, g_ref, beta_ref, of_ref, ob_ref):
    for c0 in range(0, of_ref.shape[0], LN_CHUNK):
        y = _layer_norm_rows(load_z(pl.ds(first_row + c0, LN_CHUNK)), g_ref[...], beta_ref[...])
        of_ref[c0:c0 + LN_CHUNK, :] = y
        ob_ref[c0:c0 + LN_CHUNK, :] = y.astype(ob_ref.dtype)


def _pipelined_steps(i, n_tiles, produce, consume):
    @pl.when(i == 0)
    def _():
        produce(0)

    for par in (0, 1):
        @pl.when((i > 0) & (i < n_tiles) & (i % 2 == par))
        def _():
            consume(1 - par)
            produce(par)

    @pl.when(i == n_tiles)
    def _():
        consume((n_tiles - 1) % 2)


def _conf_kernel(x_ref, w_ref, b_ref, dww_ref, dwb_ref, g_ref, beta_ref, o_ref, ubuf0, ubuf1, wtap, *, tm, n_tiles):
    c = CONF_WIDTH
    i = pl.program_id(0)
    ubufs = (ubuf0, ubuf1)

    @pl.when(i == 0)
    def _():
        for ubuf in ubufs:
            ubuf[...] = jnp.zeros(ubuf.shape, F32)
        for k in range(CONF_KERNEL):
            wtap[k] = jnp.broadcast_to(dww_ref[k:k + 1, :], (SUBLANES, c))

    def project(par):
        cur, prev = ubufs[par], ubufs[1 - par]
        xb = x_ref[...]
        for c0 in range(0, c, CONF_COLS):
            cols = slice(c0, c0 + CONF_COLS)
            gcols = slice(c + c0, c + c0 + CONF_COLS)
            pa = jnp.dot(xb, w_ref[:, cols], preferred_element_type=F32) + b_ref[:, cols]
            pg = jnp.dot(xb, w_ref[:, gcols], preferred_element_type=F32) + b_ref[:, gcols]
            u = pa * jax.nn.sigmoid(pg)
            for s in range(SUBLANES):
                cur[s, 0:CONF_HIST, cols] = prev[s, tm:tm + CONF_HIST, cols]
                cur[s, CONF_HIST - s:CONF_HIST - s + tm, cols] = u

    first_tap = CONF_HIST - (CONF_KERNEL - 1)

    def convolve(par):
        ubuf = ubufs[par]
        for base in range(0, tm, CONF_CHUNK):
            acc = jnp.broadcast_to(dwb_ref[...], (CONF_CHUNK, c))
            for k in range(CONF_KERNEL):
                s = (first_tap + k) % SUBLANES
                off = first_tap + k - s
                wk = jnp.concatenate([wtap[k]] * (CONF_CHUNK // SUBLANES), axis=0)
                acc = acc + wk * ubuf[s, base + off:base + off + CONF_CHUNK, :]
            y = _layer_norm_rows(acc, g_ref[...], beta_ref[...])
            o_ref[base:base + CONF_CHUNK, :] = (y * jax.nn.sigmoid(y)).astype(o_ref.dtype)

    _pipelined_steps(i, n_tiles, project, convolve)


def _conf_call(l, xb, w_in, b_in, dww, dwb, g, beta, *, tm=TILES.mixer_rows):
    s, d = xb.shape
    c = CONF_WIDTH
    n_tiles = s // tm
    return pl.pallas_call(
        functools.partial(_conf_kernel, tm=tm, n_tiles=n_tiles),
        out_shape=jax.ShapeDtypeStruct((s, c), BF16),
        grid=(n_tiles + 1,),
        in_specs=[
            pl.BlockSpec((tm, d), lambda i: (jnp.minimum(i, n_tiles - 1), 0)),
            *_in_cols(l, d, COL_CONF, 2 * c, resident=True),
            _resident((None, CONF_KERNEL, c), lambda i: (l, 0, 0)),
            _resident((None, 1, c), lambda i: (l, 0, 0)),
            _resident((None, 1, c), lambda i: (l, 0, 0)),
            _resident((None, 1, c), lambda i: (l, 0, 0)),
        ],
        out_specs=pl.BlockSpec((tm, c), lambda i: (jnp.maximum(i - 1, 0), 0)),
        scratch_shapes=[pltpu.VMEM((SUBLANES, tm + CONF_HIST, c), F32),
                        pltpu.VMEM((SUBLANES, tm + CONF_HIST, c), F32),
                        pltpu.VMEM((CONF_KERNEL, SUBLANES, c), F32)],
        compiler_params=_params(1),
        name="conf",
    )(xb, w_in, b_in, dww, dwb, g, beta)


def _sconv_kernel(x_ref, wb_ref, bb_ref, wc_ref, bc_ref, wx_ref, bx_ref, dww_ref, o_ref, vbuf, pbuf, *, tm):
    c = SC_WIDTH

    @pl.when(pl.program_id(0) == 0)
    def _():
        vbuf[0:SC_HIST, :] = jnp.zeros((SC_HIST, c), F32)

    xb = x_ref[...]
    pbuf[...] = jnp.dot(xb, wb_ref[...], preferred_element_type=F32) + bb_ref[...]
    pc = jnp.dot(xb, wc_ref[...], preferred_element_type=F32) + bc_ref[...]
    px = jnp.dot(xb, wx_ref[...], preferred_element_type=F32) + bx_ref[...]
    vbuf[SC_HIST:SC_HIST + tm, :] = pc * px

    first_tap = SC_HIST - (SC_KERNEL - 1)

    def chunk(ci, carry):
        base = pl.multiple_of(ci * SC_CHUNK, SC_CHUNK)
        win = vbuf[pl.ds(base, SC_CHUNK + SC_HIST), :]
        acc = dww_ref[0:1, :] * win[first_tap:first_tap + SC_CHUNK, :]
        for k in range(1, SC_KERNEL):
            acc = acc + dww_ref[k:k + 1, :] * win[first_tap + k:first_tap + k + SC_CHUNK, :]
        o_ref[pl.ds(base, SC_CHUNK), :] = (pbuf[pl.ds(base, SC_CHUNK), :] * acc).astype(o_ref.dtype)
        return carry

    lax.fori_loop(0, tm // SC_CHUNK, chunk, 0)
    vbuf[0:SC_HIST, :] = vbuf[tm:tm + SC_HIST, :]


def _sconv_call(l, xb, w_in, b_in, dww, *, tm=TILES.sconv_rows):
    s, d = xb.shape
    c = SC_WIDTH
    wb, bb = _in_cols(l, d, COL_SC, c, resident=True)
    wc, bc = _in_cols(l, d, COL_SC + c, c, resident=True)
    wx, bx = _in_cols(l, d, COL_SC + 2 * c, c, resident=True)
    return pl.pallas_call(
        functools.partial(_sconv_kernel, tm=tm),
        out_shape=jax.ShapeDtypeStruct((s, c), BF16),
        grid=(s // tm,),
        in_specs=[
            pl.BlockSpec((tm, d), lambda i: (i, 0)),
            wb, bb, wc, bc, wx, bx,
            _resident((None, SC_KERNEL, c), lambda i: (l, 0, 0)),
        ],
        out_specs=pl.BlockSpec((tm, c), lambda i: (i, 0)),
        scratch_shapes=[pltpu.VMEM((tm + SC_HIST, c), F32), pltpu.VMEM((tm, c), F32)],
        compiler_params=_params(1),
        name="sconv",
    )(xb, w_in, b_in, w_in, b_in, w_in, b_in, dww)


def _attn_kernel(x_ref, wq0_ref, bq0_ref, wq1_ref, bq1_ref, wk_ref, bk_ref, wv_ref, bv_ref, sink_ref, slope_ref,
                 o_ref, q_s, klo_s, khi_s, vlo_s, vhi_s, bias_s, *, tm, n_tiles):
    i = pl.program_id(0)
    half_w = ATTN_WIDTH // 2
    slab = 2 * HEAD_DIM
    q_slabs_per_kv = GQA_GROUP // 2
    low_lanes = lax.broadcasted_iota(jnp.int32, (tm, slab), 1) < HEAD_DIM
    low_lanes_block = lax.broadcasted_iota(jnp.int32, (BLOCK, slab), 1) < HEAD_DIM

    qi = lax.broadcasted_iota(jnp.int32, (BLOCK, 2 * BLOCK), 0)
    si = lax.broadcasted_iota(jnp.int32, (BLOCK, 2 * BLOCK), 1)
    dist_i = BLOCK + qi - si
    in_window = (dist_i >= 0) & (dist_i < WINDOW)

    @pl.when(i == 0)
    def _():
        dist = dist_i.astype(F32)
        for hq in range(N_Q_HEADS):
            bias_s[hq] = -slope_ref[hq] * dist
        for ref in (klo_s, khi_s, vlo_s, vhi_s):
            ref[1, tm:tm + BLOCK, :] = jnp.zeros((BLOCK, N_KV_HEADS * slab), BF16)

    def split_heads(t, lo_ref, hi_ref, cur):
        zero = jnp.zeros((tm, slab), F32)
        for p in range(N_KV_HEADS // 2):
            pair = t[:, p * slab:(p + 1) * slab]
            swapped = pltpu.roll(pair, HEAD_DIM, axis=1)
            even, odd = 2 * p, 2 * p + 1
            lo_ref[cur, BLOCK:BLOCK + tm, even * slab:(even + 1) * slab] = jnp.where(low_lanes, pair, zero).astype(BF16)
            hi_ref[cur, BLOCK:BLOCK + tm, even * slab:(even + 1) * slab] = jnp.where(low_lanes, zero, swapped).astype(BF16)
            lo_ref[cur, BLOCK:BLOCK + tm, odd * slab:(odd + 1) * slab] = jnp.where(low_lanes, swapped, zero).astype(BF16)
            hi_ref[cur, BLOCK:BLOCK + tm, odd * slab:(odd + 1) * slab] = jnp.where(low_lanes, zero, pair).astype(BF16)

    def project(cur):
        prev = 1 - cur
        xb = x_ref[...]
        scale = HEAD_DIM ** -0.5
        q0 = (jnp.dot(xb, wq0_ref[...], preferred_element_type=F32) + bq0_ref[...]) * scale
        q_s[cur, :, :half_w] = q0.astype(BF16)
        q1 = (jnp.dot(xb, wq1_ref[...], preferred_element_type=F32) + bq1_ref[...]) * scale
        q_s[cur, :, half_w:] = q1.astype(BF16)
        split_heads(jnp.dot(xb, wk_ref[...], preferred_element_type=F32) + bk_ref[...], klo_s, khi_s, cur)
        split_heads(jnp.dot(xb, wv_ref[...], preferred_element_type=F32) + bv_ref[...], vlo_s, vhi_s, cur)
        for ref in (klo_s, khi_s, vlo_s, vhi_s):
            ref[cur, 0:BLOCK, :] = ref[prev, tm:tm + BLOCK, :]

    def attend(prev):
        for jb in range(tm // BLOCK):
            r0 = jb * BLOCK
            if jb == 0:
                mask = in_window & (si >= jnp.where(i == 1, BLOCK, 0))
            else:
                mask = in_window
            for h in range(N_KV_HEADS):
                cols = slice(h * slab, (h + 1) * slab)
                band = slice(r0, r0 + 2 * BLOCK)
                kcat = jnp.concatenate([klo_s[prev, band, cols], khi_s[prev, band, cols]], axis=0)
                vcat = jnp.concatenate([vlo_s[prev, band, cols], vhi_s[prev, band, cols]], axis=0)
                first_slab = h * q_slabs_per_kv
                qst = jnp.concatenate(
                    [q_s[prev, r0:r0 + BLOCK, (first_slab + s) * slab:(first_slab + s + 1) * slab]
                     for s in range(q_slabs_per_kv)], axis=0)
                scores = lax.dot_general(qst, kcat, (((1,), (1,)), ((), ())), preferred_element_type=F32)
                e_rows, den_rows = [], []
                for s in range(q_slabs_per_kv):
                    e_cols, dens = [], []
                    for odd in range(2):
                        hq = 2 * (first_slab + s) + odd
                        blk = scores[s * BLOCK:(s + 1) * BLOCK, odd * 2 * BLOCK:(odd + 1) * 2 * BLOCK]
                        sc = jnp.where(mask, blk + bias_s[hq], -jnp.inf)
                        sink = sink_ref[hq]
                        m = jnp.maximum(jnp.max(sc, axis=-1, keepdims=True), sink)
                        e = jnp.exp(sc - m)
                        dens.append(jnp.sum(e, axis=-1, keepdims=True) + jnp.exp(sink - m))
                        e_cols.append(e.astype(BF16))
                    e_rows.append(jnp.concatenate(e_cols, axis=1))
                    den_rows.append(jnp.where(low_lanes_block, dens[0], dens[1]))
                pv = jnp.dot(jnp.concatenate(e_rows, axis=0), vcat, preferred_element_type=F32)
                out = pv / jnp.concatenate(den_rows, axis=0)
                for s in range(q_slabs_per_kv):
                    o_ref[r0:r0 + BLOCK, (first_slab + s) * slab:(first_slab + s + 1) * slab] = (
                        out[s * BLOCK:(s + 1) * BLOCK, :].astype(o_ref.dtype))

    _pipelined_steps(i, n_tiles, project, attend)


def _attn_call(l, xb, w_in, b_in, sinks, slopes, *, tm=TILES.mixer_rows):
    s, d = xb.shape
    half = ATTN_WIDTH // 2
    n_tiles = s // tm
    assert GQA_GROUP % 2 == 0 and N_KV_HEADS % 2 == 0
    wq0, bq0 = _in_cols(l, d, COL_Q, half, resident=True)
    wq1, bq1 = _in_cols(l, d, COL_Q + half, half, resident=True)
    wk, bk = _in_cols(l, d, COL_K, KV_WIDTH, resident=True)
    wv, bv = _in_cols(l, d, COL_V, KV_WIDTH, resident=True)
    kv_scratch = pltpu.VMEM((2, tm + BLOCK, N_KV_HEADS * 2 * HEAD_DIM), BF16)
    return pl.pallas_call(
        functools.partial(_attn_kernel, tm=tm, n_tiles=n_tiles),
        out_shape=jax.ShapeDtypeStruct((s, ATTN_WIDTH), BF16),
        grid=(n_tiles + 1,),
        in_specs=[
            pl.BlockSpec((tm, d), lambda i: (jnp.minimum(i, n_tiles - 1), 0)),
            wq0, bq0, wq1, bq1, wk, bk, wv, bv,
            pl.BlockSpec(memory_space=pltpu.SMEM),
            pl.BlockSpec(memory_space=pltpu.SMEM),
        ],
        out_specs=pl.BlockSpec((tm, ATTN_WIDTH), lambda i: (jnp.maximum(i - 1, 0), 0)),
        scratch_shapes=[
            pltpu.VMEM((2, tm, ATTN_WIDTH), BF16),
            kv_scratch, kv_scratch, kv_scratch, kv_scratch,
            pltpu.VMEM((N_Q_HEADS, BLOCK, 2 * BLOCK), F32),
        ],
        compiler_params=_params(1),
        name="attn",
    )(xb, w_in, b_in, w_in, b_in, w_in, b_in, w_in, b_in, sinks, slopes)


def _merge_kernel(x_ref, wga_ref, wgb_ref, wgc_ref, bga_ref, bgb_ref, bgc_ref,
                  ya_ref, yb_ref, yc_ref, wba_ref, wbb_ref, wbc_ref, o_ref, wg_s, wb_s):
    row_starts = (0, CONF_WIDTH, CONF_WIDTH + SC_WIDTH, MIX_WIDTH)

    @pl.when(pl.program_id(1) == 0)
    def _():
        for gi, wg_ref in enumerate((wga_ref, wgb_ref, wgc_ref)):
            wg_s[gi] = wg_ref[...].astype(BF16)
        for bi, wb_ref in enumerate((wba_ref, wbb_ref, wbc_ref)):
            wb_s[row_starts[bi]:row_starts[bi + 1], :] = wb_ref[...].astype(BF16)

    xb = x_ref[...]

    def gated(gi, bg_ref, y_ref):
        gate = jax.nn.sigmoid(jnp.dot(xb, wg_s[gi], preferred_element_type=F32) + bg_ref[...])
        wb = wb_s[row_starts[gi]:row_starts[gi + 1], :]
        return gate * jnp.dot(y_ref[...], wb, preferred_element_type=F32)

    merged = gated(0, bga_ref, ya_ref)
    merged = merged + gated(1, bgb_ref, yb_ref)
    merged = merged + gated(2, bgc_ref, yc_ref)
    o_ref[...] = merged.astype(o_ref.dtype)


def _merge_call(l, xb, w_in, b_in, ya, yb, yc, w_branch, *, tm=TILES.merge_rows, tn=TILES.merge_cols):
    s, d = xb.shape
    nj = d // tn
    assert COL_GATE % tn == 0
    gate0 = COL_GATE // tn
    assert SC_WIDTH == CONF_WIDTH and ATTN_WIDTH == CONF_WIDTH + SC_WIDTH

    n_tiles = s // tm

    def ahead(j, i, lead):
        return jnp.minimum(jnp.where(i >= max(1, n_tiles - lead), j + 1, j), nj - 1)

    def wg_spec(gi):
        return pl.BlockSpec((None, d, tn), lambda j, i: (l, 0, gate0 + gi * nj + ahead(j, i, N_GATES + 1 - gi)))

    def bg_spec(gi):
        return pl.BlockSpec((None, 1, tn), lambda j, i: (l, 0, gate0 + gi * nj + j))

    return pl.pallas_call(
        _merge_kernel,
        out_shape=jax.ShapeDtypeStruct((s, d), BF16),
        grid=(nj, s // tm),
        in_specs=[
            pl.BlockSpec((tm, d), lambda j, i: (i, 0)),
            wg_spec(0), wg_spec(1), wg_spec(2),
            bg_spec(0), bg_spec(1), bg_spec(2),
            pl.BlockSpec((tm, CONF_WIDTH), lambda j, i: (i, 0)),
            pl.BlockSpec((tm, SC_WIDTH), lambda j, i: (i, 0)),
            pl.BlockSpec((tm, ATTN_WIDTH), lambda j, i: (i, 0)),
            pl.BlockSpec((None, CONF_WIDTH, tn), lambda j, i: (l, 0, ahead(j, i, 1))),
            pl.BlockSpec((None, SC_WIDTH, tn), lambda j, i: (l, 1, ahead(j, i, 1))),
            pl.BlockSpec((None, ATTN_WIDTH, tn), lambda j, i: (l, 1, ahead(j, i, 1))),
        ],
        out_specs=pl.BlockSpec((tm, tn), lambda j, i: (i, j)),
        scratch_shapes=[pltpu.VMEM((N_GATES, d, tn), BF16), pltpu.VMEM((MIX_WIDTH, tn), BF16)],
        compiler_params=_params(2),
        name="merge",
    )(xb, w_in, w_in, w_in, b_in, b_in, b_in, ya, yb, yc, w_branch, w_branch, w_branch)


def _out_ln_kernel(a_ref, w_ref, b_ref, res_ref, g_ref, beta_ref, of_ref, ob_ref, z_s, *, nn, n_tiles, alpha):
    i = pl.program_id(0)
    j = pl.program_id(1)
    slab_rows = of_ref.shape[0]

    def matmul_step(par):
        z = jnp.dot(a_ref[...], w_ref[j], preferred_element_type=F32) + b_ref[...]
        z_s[par, j] = z + alpha * res_ref[...]

    def ln_step(par):
        def load_z(rows):
            return jnp.concatenate([z_s[par, jj, rows, :] for jj in range(nn)], axis=-1)

        _ln_row_slab(load_z, pl.multiple_of(j * slab_rows, slab_rows), g_ref, beta_ref, of_ref, ob_ref)

    _pipelined_steps(i, n_tiles, matmul_step, ln_step)


def _out_ln_call(l, a, w_slabs, b, res, g, beta, *, alpha, tm=TILES.out_ln_rows):
    s, kdim = a.shape
    nn, tn = w_slabs.shape[1], w_slabs.shape[3]
    d = nn * tn
    n_tiles = s // tm
    slab_rows = tm // nn
    assert slab_rows % (2 * SUBLANES) == 0

    def tile(i):
        return jnp.minimum(i, n_tiles - 1)

    def col(i, j):
        return jnp.where(i == n_tiles, nn - 1, j)

    def out_slab(i, j):
        return (jnp.where(i == 0, 0, (i - 1) * nn + j), 0)

    return pl.pallas_call(
        functools.partial(_out_ln_kernel, nn=nn, n_tiles=n_tiles, alpha=alpha),
        out_shape=(jax.ShapeDtypeStruct((s, d), F32), jax.ShapeDtypeStruct((s, d), BF16)),
        grid=(n_tiles + 1, nn),
        in_specs=[
            pl.BlockSpec((tm, kdim), lambda i, j: (tile(i), 0)),
            _resident((None, nn, kdim, tn), lambda i, j: (l, 0, 0, 0)),
            pl.BlockSpec((None, 1, tn), lambda i, j: (l, 0, col(i, j))),
            pl.BlockSpec((tm, tn), lambda i, j: (tile(i), col(i, j))),
            pl.BlockSpec((None, 1, d), lambda i, j: (l, 0, 0)),
            pl.BlockSpec((None, 1, d), lambda i, j: (l, 0, 0)),
        ],
        out_specs=(pl.BlockSpec((slab_rows, d), out_slab), pl.BlockSpec((slab_rows, d), out_slab)),
        scratch_shapes=[pltpu.VMEM((2, nn, tm, tn), F32)],
        compiler_params=_params(2),
        name="out_ln",
    )(a, w_slabs, b, res, g, beta)


def _mlp2_ln_kernel(a_ref, w_ref, b_ref, res_ref, g_ref, beta_ref, of_ref, ob_ref, z_s, *, nk, n_tiles, alpha):
    i = pl.program_id(0)
    kk = pl.program_id(1)
    d = z_s.shape[-1]
    slab_rows = of_ref.shape[0]
    slab = pl.ds(pl.multiple_of(kk * slab_rows, slab_rows), slab_rows)
    slab_w = min(MM_LN_SLAB, d)

    def matmul_step(par):
        for n0 in range(0, d, slab_w):
            part = jnp.dot(a_ref[...], w_ref[:, n0:n0 + slab_w].astype(BF16), preferred_element_type=F32)
            z_s[par, :, n0:n0 + slab_w] = jnp.where(kk == 0, part, z_s[par, :, n0:n0 + slab_w] + part)
        z_s[par, slab, :] += alpha * res_ref[...] + b_ref[...]

    def ln_step(par):
        _ln_row_slab(lambda rows: z_s[par, rows, :], pl.multiple_of(kk * slab_rows, slab_rows),
                     g_ref, beta_ref, of_ref, ob_ref)

    @pl.when((i == 0) & (kk == 0))
    def _():
        z_s[...] = jnp.zeros(z_s.shape, F32)

    _pipelined_steps(i, n_tiles, matmul_step, ln_step)


def _mlp2_ln_call(l, a, w, b, res, g, beta, *, alpha, tm=TILES.mlp2_rows, tk=TILES.mlp2_k):
    s, kdim = a.shape
    d = w.shape[-1]
    nk = kdim // tk
    n_tiles = s // tm
    slab_rows = tm // nk
    assert tm % nk == 0 and slab_rows % (2 * SUBLANES) == 0

    def tile(i):
        return jnp.minimum(i, n_tiles - 1)

    def kstep(i, k):
        return jnp.where(i == n_tiles, nk - 1, k)

    def out_slab(i, k):
        return (jnp.where(i == 0, 0, (i - 1) * nk + k), 0)

    return pl.pallas_call(
        functools.partial(_mlp2_ln_kernel, nk=nk, n_tiles=n_tiles, alpha=alpha),
        out_shape=(jax.ShapeDtypeStruct((s, d), F32), jax.ShapeDtypeStruct((s, d), BF16)),
        grid=(n_tiles + 1, nk),
        in_specs=[
            pl.BlockSpec((tm, tk), lambda i, k: (tile(i), kstep(i, k))),
            pl.BlockSpec((None, tk, d), lambda i, k: (l, kstep(i, k), 0)),
            pl.BlockSpec((None, 1, d), lambda i, k: (l, 0, 0)),
            pl.BlockSpec((slab_rows, d), lambda i, k: (tile(i) * nk + kstep(i, k), 0)),
            pl.BlockSpec((None, 1, d), lambda i, k: (l, 0, 0)),
            pl.BlockSpec((None, 1, d), lambda i, k: (l, 0, 0)),
        ],
        out_specs=(pl.BlockSpec((slab_rows, d), out_slab), pl.BlockSpec((slab_rows, d), out_slab)),
        scratch_shapes=[pltpu.VMEM((2, tm, d), F32)],
        compiler_params=_params(2),
        name="mlp2_ln",
    )(a, w, b, res, g, beta)


def _mlp1_kernel(x_ref, w_ref, b_ref, o_ref):
    h = jnp.dot(x_ref[...], w_ref[...].astype(BF16), preferred_element_type=F32) + b_ref[...]
    h = jnp.maximum(h, 0.0)
    o_ref[...] = (h * h).astype(o_ref.dtype)


def _mlp1_call(l, xb, w, b, *, tm=TILES.mlp1_rows, tn=TILES.mlp1_cols):
    s, d = xb.shape
    f = w.shape[-1]
    return pl.pallas_call(
        _mlp1_kernel,
        out_shape=jax.ShapeDtypeStruct((s, f), BF16),
        grid=(s // tm, f // tn),
        in_specs=[
            pl.BlockSpec((tm, d), lambda i, j: (i, 0)),
            pl.BlockSpec((None, d, tn), lambda i, j: (l, 0, j)),
            pl.BlockSpec((None, 1, tn), lambda i, j: (l, 0, j)),
        ],
        out_specs=pl.BlockSpec((tm, tn), lambda i, j: (i, j)),
        compiler_params=_params(2),
        name="mlp1",
    )(xb, w, b)


def kernel(x, w_in, b_in, conf_dw_w, conf_dw_b, conf_ln_g, conf_ln_b, sc_dw_w, attn_sinks, w_branch, w_out, b_out,
           ln1_g, ln1_b, w_mlp1, b_mlp1, w_mlp2, b_mlp2, ln2_g, ln2_b):
    batch, seq, d_model = x.shape
    depth = w_in.shape[0]
    alpha = float((2 * depth) ** 0.25)
    assert batch == 1
    assert w_in.shape[-1] == COL_GATE + N_GATES * d_model

    def row(v):
        return v[:, None, :]

    w_in_b = w_in[:, :, :COL_GATE].astype(BF16)
    out_slab = min(TILES.out_ln_cols, d_model)
    w_out_b = w_out.astype(BF16).reshape(depth, d_model, d_model // out_slab, out_slab).transpose(0, 2, 1, 3)
    b_in_r = row(b_in)
    slopes = jnp.asarray(_alibi_slopes(N_Q_HEADS))

    xf = x.reshape(seq, d_model)
    xb = xf.astype(BF16)
    for l in range(depth):
        ya = _conf_call(l, xb, w_in_b, b_in_r, conf_dw_w, row(conf_dw_b), row(conf_ln_g), row(conf_ln_b))
        yb = _sconv_call(l, xb, w_in_b, b_in_r, sc_dw_w)
        yc = _attn_call(l, xb, w_in_b, b_in_r, attn_sinks[l], slopes)
        merged = _merge_call(l, xb, w_in, b_in_r, ya, yb, yc, w_branch)
        xf, xb = _out_ln_call(l, merged, w_out_b, row(b_out), xf, row(ln1_g), row(ln1_b), alpha=alpha)
        hid = _mlp1_call(l, xb, w_mlp1, row(b_mlp1))
        xf, xb = _mlp2_ln_call(l, hid, w_mlp2, row(b_mlp2), xf, row(ln2_g), row(ln2_b), alpha=alpha)
    return xf.reshape(batch, seq, d_model)
```

```python
import functools
from typing import NamedTuple

import jax
import jax.numpy as jnp
import numpy as np
from jax import lax
from jax.experimental import pallas as pl
from jax.experimental.pallas import tpu as pltpu

F32 = jnp.float32
BF16 = jnp.bfloat16

CONF_WIDTH = 768
CONF_KERNEL = 31
SC_WIDTH = 768
SC_KERNEL = 3
HEAD_DIM = 64
N_Q_HEADS = 24
N_KV_HEADS = 4
GQA_GROUP = N_Q_HEADS // N_KV_HEADS
ATTN_WIDTH = N_Q_HEADS * HEAD_DIM
KV_WIDTH = N_KV_HEADS * HEAD_DIM
MIX_WIDTH = CONF_WIDTH + SC_WIDTH + ATTN_WIDTH
WINDOW = 128
BLOCK = 128
N_GATES = 3
LN_EPS = 1e-5

COL_CONF = 0
COL_SC = COL_CONF + 2 * CONF_WIDTH
COL_Q = COL_SC + 3 * SC_WIDTH
COL_K = COL_Q + ATTN_WIDTH
COL_V = COL_K + KV_WIDTH
COL_GATE = COL_V + KV_WIDTH

SUBLANES = 8

V7X_VMEM_BYTES = 64 * 1024 * 1024
VMEM_LIMIT_BYTES = V7X_VMEM_BYTES - 4 * 1024 * 1024


class _Tiles(NamedTuple):
    mixer_rows: int = 256
    sconv_rows: int = 512
    merge_rows: int = 512
    merge_cols: int = 256
    out_ln_rows: int = 256
    out_ln_cols: int = 2048
    mlp1_rows: int = 2048
    mlp1_cols: int = 512
    mlp2_rows: int = 1024
    mlp2_k: int = 512


TILES = _Tiles()

CONF_HIST = 32
CONF_CHUNK = 32
CONF_COLS = 256
SC_HIST = 8
SC_CHUNK = 64
LN_CHUNK = 8
MM_LN_SLAB = 1024


def _alibi_slopes(n):
    def pow2(m):
        start = 2.0 ** (-8.0 / m)
        return [start ** (i + 1) for i in range(m)]
    if (n & (n - 1)) == 0:
        s = pow2(n)
    else:
        c = 2 ** int(np.floor(np.log2(n)))
        s = pow2(c) + pow2(2 * c)[0::2][: n - c]
    return np.array(s, dtype=np.float32)


def _params(n_axes):
    return pltpu.CompilerParams(
        dimension_semantics=("arbitrary",) * n_axes,
        vmem_limit_bytes=VMEM_LIMIT_BYTES,
    )


def _resident(block_shape, index_map):
    return pl.BlockSpec(block_shape, index_map, pipeline_mode=pl.Buffered(1))


def _in_cols(l, d, col, width, *, resident):
    assert col % width == 0
    make = _resident if resident else pl.BlockSpec
    return (make((None, d, width), lambda *_: (l, 0, col // width)),
            make((None, 1, width), lambda *_: (l, 0, col // width)))


def _layer_norm_rows(z, g, b):
    mu = jnp.mean(z, axis=-1, keepdims=True)
    zc = z - mu
    var = jnp.mean(zc * zc, axis=-1, keepdims=True)
    return zc * lax.rsqrt(var + LN_EPS) * g + b


def _ln_row_slab(load_z, first_row, g_ref, beta_ref, of_ref, ob_ref):
    for c0 in range(0, of_ref.shape[0], LN_CHUNK):
        y = _layer_norm_rows(load_z(pl.ds(first_row + c0, LN_CHUNK)), g_ref[...], beta_ref[...])
        of_ref[c0:c0 + LN_CHUNK, :] = y
        ob_ref[c0:c0 + LN_CHUNK, :] = y.astype(ob_ref.dtype)


def _pipelined_steps(i, n_tiles, produce, consume):
    @pl.when(i == 0)
    def _():
        produce(0)

    for par in (0, 1):
        @pl.when((i > 0) & (i < n_tiles) & (i % 2 == par))
        def _():
            consume(1 - par)
            produce(par)

    @pl.when(i == n_tiles)
    def _():
        consume((n_tiles - 1) % 2)


def _conf_kernel(x_ref, w_ref, b_ref, dww_ref, dwb_ref, g_ref, beta_ref, o_ref, ubuf0, ubuf1, wtap, *, tm, n_tiles):
    c = CONF_WIDTH
    i = pl.program_id(0)
    ubufs = (ubuf0, ubuf1)

    @pl.when(i == 0)
    def _():
        for ubuf in ubufs:
            ubuf[...] = jnp.zeros(ubuf.shape, F32)
        for k in range(CONF_KERNEL):
            wtap[k] = jnp.broadcast_to(dww_ref[k:k + 1, :], (SUBLANES, c))

    def project(par):
        cur, prev = ubufs[par], ubufs[1 - par]
        xb = x_ref[...]
        for c0 in range(0, c, CONF_COLS):
            cols = slice(c0, c0 + CONF_COLS)
            gcols = slice(c + c0, c + c0 + CONF_COLS)
            pa = jnp.dot(xb, w_ref[:, cols], preferred_element_type=F32) + b_ref[:, cols]
            pg = jnp.dot(xb, w_ref[:, gcols], preferred_element_type=F32) + b_ref[:, gcols]
            u = pa * jax.nn.sigmoid(pg)
            for s in range(SUBLANES):
                cur[s, 0:CONF_HIST, cols] = prev[s, tm:tm + CONF_HIST, cols]
                cur[s, CONF_HIST - s:CONF_HIST - s + tm, cols] = u

    first_tap = CONF_HIST - (CONF_KERNEL - 1)

    def convolve(par):
        ubuf = ubufs[par]
        for base in range(0, tm, CONF_CHUNK):
            acc = jnp.broadcast_to(dwb_ref[...], (CONF_CHUNK, c))
            for k in range(CONF_KERNEL):
                s = (first_tap + k) % SUBLANES
                off = first_tap + k - s
                wk = jnp.concatenate([wtap[k]] * (CONF_CHUNK // SUBLANES), axis=0)
                acc = acc + wk * ubuf[s, base + off:base + off + CONF_CHUNK, :]
            y = _layer_norm_rows(acc, g_ref[...], beta_ref[...])
            o_ref[base:base + CONF_CHUNK, :] = (y * jax.nn.sigmoid(y)).astype(o_ref.dtype)

    _pipelined_steps(i, n_tiles, project, convolve)


def _conf_call(l, xb, w_in, b_in, dww, dwb, g, beta, *, tm=TILES.mixer_rows):
    s, d = xb.shape
    c = CONF_WIDTH
    n_tiles = s // tm
    return pl.pallas_call(
        functools.partial(_conf_kernel, tm=tm, n_tiles=n_tiles),
        out_shape=jax.ShapeDtypeStruct((s, c), BF16),
        grid=(n_tiles + 1,),
        in_specs=[
            pl.BlockSpec((tm, d), lambda i: (jnp.minimum(i, n_tiles - 1), 0)),
            *_in_cols(l, d, COL_CONF, 2 * c, resident=True),
            _resident((None, CONF_KERNEL, c), lambda i: (l, 0, 0)),
            _resident((None, 1, c), lambda i: (l, 0, 0)),
            _resident((None, 1, c), lambda i: (l, 0, 0)),
            _resident((None, 1, c), lambda i: (l, 0, 0)),
        ],
        out_specs=pl.BlockSpec((tm, c), lambda i: (jnp.maximum(i - 1, 0), 0)),
        scratch_shapes=[pltpu.VMEM((SUBLANES, tm + CONF_HIST, c), F32),
                        pltpu.VMEM((SUBLANES, tm + CONF_HIST, c), F32),
                        pltpu.VMEM((CONF_KERNEL, SUBLANES, c), F32)],
        compiler_params=_params(1),
        name="conf",
    )(xb, w_in, b_in, dww, dwb, g, beta)


def _sconv_kernel(x_ref, wb_ref, bb_ref, wc_ref, bc_ref, wx_ref, bx_ref, dww_ref, o_ref, vbuf, pbuf, *, tm):
    c = SC_WIDTH

    @pl.when(pl.program_id(0) == 0)
    def _():
        vbuf[0:SC_HIST, :] = jnp.zeros((SC_HIST, c), F32)

    xb = x_ref[...]
    pbuf[...] = jnp.dot(xb, wb_ref[...], preferred_element_type=F32) + bb_ref[...]
    pc = jnp.dot(xb, wc_ref[...], preferred_element_type=F32) + bc_ref[...]
    px = jnp.dot(xb, wx_ref[...], preferred_element_type=F32) + bx_ref[...]
    vbuf[SC_HIST:SC_HIST + tm, :] = pc * px

    first_tap = SC_HIST - (SC_KERNEL - 1)

    def chunk(ci, carry):
        base = pl.multiple_of(ci * SC_CHUNK, SC_CHUNK)
        win = vbuf[pl.ds(base, SC_CHUNK + SC_HIST), :]
        acc = dww_ref[0:1, :] * win[first_tap:first_tap + SC_CHUNK, :]
        for k in range(1, SC_KERNEL):
            acc = acc + dww_ref[k:k + 1, :] * win[first_tap + k:first_tap + k + SC_CHUNK, :]
        o_ref[pl.ds(base, SC_CHUNK), :] = (pbuf[pl.ds(base, SC_CHUNK), :] * acc).astype(o_ref.dtype)
        return carry

    lax.fori_loop(0, tm // SC_CHUNK, chunk, 0)
    vbuf[0:SC_HIST, :] = vbuf[tm:tm + SC_HIST, :]


def _sconv_call(l, xb, w_in, b_in, dww, *, tm=TILES.sconv_rows):
    s, d = xb.shape
    c = SC_WIDTH
    wb, bb = _in_cols(l, d, COL_SC, c, resident=True)
    wc, bc = _in_cols(l, d, COL_SC + c, c, resident=True)
    wx, bx = _in_cols(l, d, COL_SC + 2 * c, c, resident=True)
    return pl.pallas_call(
        functools.partial(_sconv_kernel, tm=tm),
        out_shape=jax.ShapeDtypeStruct((s, c), BF16),
        grid=(s // tm,),
        in_specs=[
            pl.BlockSpec((tm, d), lambda i: (i, 0)),
            wb, bb, wc, bc, wx, bx,
            _resident((None, SC_KERNEL, c), lambda i: (l, 0, 0)),
        ],
        out_specs=pl.BlockSpec((tm, c), lambda i: (i, 0)),
        scratch_shapes=[pltpu.VMEM((tm + SC_HIST, c), F32), pltpu.VMEM((tm, c), F32)],
        compiler_params=_params(1),
        name="sconv",
    )(xb, w_in, b_in, w_in, b_in, w_in, b_in, dww)


def _attn_kernel(x_ref, wq0_ref, bq0_ref, wq1_ref, bq1_ref, wk_ref, bk_ref, wv_ref, bv_ref, sink_ref, slope_ref,
                 o_ref, q_s, klo_s, khi_s, vlo_s, vhi_s, bias_s, *, tm, n_tiles):
    i = pl.program_id(0)
    half_w = ATTN_WIDTH // 2
    slab = 2 * HEAD_DIM
    q_slabs_per_kv = GQA_GROUP // 2
    low_lanes = lax.broadcasted_iota(jnp.int32, (tm, slab), 1) < HEAD_DIM
    low_lanes_block = lax.broadcasted_iota(jnp.int32, (BLOCK, slab), 1) < HEAD_DIM

    qi = lax.broadcasted_iota(jnp.int32, (BLOCK, 2 * BLOCK), 0)
    si = lax.broadcasted_iota(jnp.int32, (BLOCK, 2 * BLOCK), 1)
    dist_i = BLOCK + qi - si
    in_window = (dist_i >= 0) & (dist_i < WINDOW)

    @pl.when(i == 0)
    def _():
        dist = dist_i.astype(F32)
        for hq in range(N_Q_HEADS):
            bias_s[hq] = -slope_ref[hq] * dist
        for ref in (klo_s, khi_s, vlo_s, vhi_s):
            ref[1, tm:tm + BLOCK, :] = jnp.zeros((BLOCK, N_KV_HEADS * slab), BF16)

    def split_heads(t, lo_ref, hi_ref, cur):
        zero = jnp.zeros((tm, slab), F32)
        for p in range(N_KV_HEADS // 2):
            pair = t[:, p * slab:(p + 1) * slab]
            swapped = pltpu.roll(pair, HEAD_DIM, axis=1)
            even, odd = 2 * p, 2 * p + 1
            lo_ref[cur, BLOCK:BLOCK + tm, even * slab:(even + 1) * slab] = jnp.where(low_lanes, pair, zero).astype(BF16)
            hi_ref[cur, BLOCK:BLOCK + tm, even * slab:(even + 1) * slab] = jnp.where(low_lanes, zero, swapped).astype(BF16)
            lo_ref[cur, BLOCK:BLOCK + tm, odd * slab:(odd + 1) * slab] = jnp.where(low_lanes, swapped, zero).astype(BF16)
            hi_ref[cur, BLOCK:BLOCK + tm, odd * slab:(odd + 1) * slab] = jnp.where(low_lanes, zero, pair).astype(BF16)

    def project(cur):
        prev = 1 - cur
        xb = x_ref[...]
        scale = HEAD_DIM ** -0.5
        q0 = (jnp.dot(xb, wq0_ref[...], preferred_element_type=F32) + bq0_ref[...]) * scale
        q_s[cur, :, :half_w] = q0.astype(BF16)
        q1 = (jnp.dot(xb, wq1_ref[...], preferred_element_type=F32) + bq1_ref[...]) * scale
        q_s[cur, :, half_w:] = q1.astype(BF16)
        split_heads(jnp.dot(xb, wk_ref[...], preferred_element_type=F32) + bk_ref[...], klo_s, khi_s, cur)
        split_heads(jnp.dot(xb, wv_ref[...], preferred_element_type=F32) + bv_ref[...], vlo_s, vhi_s, cur)
        for ref in (klo_s, khi_s, vlo_s, vhi_s):
            ref[cur, 0:BLOCK, :] = ref[prev, tm:tm + BLOCK, :]

    def attend(prev):
        for jb in range(tm // BLOCK):
            r0 = jb * BLOCK
            if jb == 0:
                mask = in_window & (si >= jnp.where(i == 1, BLOCK, 0))
            else:
                mask = in_window
            for h in range(N_KV_HEADS):
                cols = slice(h * slab, (h + 1) * slab)
                band = slice(r0, r0 + 2 * BLOCK)
                kcat = jnp.concatenate([klo_s[prev, band, cols], khi_s[prev, band, cols]], axis=0)
                vcat = jnp.concatenate([vlo_s[prev, band, cols], vhi_s[prev, band, cols]], axis=0)
                first_slab = h * q_slabs_per_kv
                qst = jnp.concatenate(
                    [q_s[prev, r0:r0 + BLOCK, (first_slab + s) * slab:(first_slab + s + 1) * slab]
                     for s in range(q_slabs_per_kv)], axis=0)
                scores = lax.dot_general(qst, kcat, (((1,), (1,)), ((), ())), preferred_element_type=F32)
                e_rows, den_rows = [], []
                for s in range(q_slabs_per_kv):
                    e_cols, dens = [], []
                    for odd in range(2):
                        hq = 2 * (first_slab + s) + odd
                        blk = scores[s * BLOCK:(s + 1) * BLOCK, odd * 2 * BLOCK:(odd + 1) * 2 * BLOCK]
                        sc = jnp.where(mask, blk + bias_s[hq], -jnp.inf)
                        sink = sink_ref[hq]
                        m = jnp.maximum(jnp.max(sc, axis=-1, keepdims=True), sink)
                        e = jnp.exp(sc - m)
                        dens.append(jnp.sum(e, axis=-1, keepdims=True) + jnp.exp(sink - m))
                        e_cols.append(e.astype(BF16))
                    e_rows.append(jnp.concatenate(e_cols, axis=1))
                    den_rows.append(jnp.where(low_lanes_block, dens[0], dens[1]))
                pv = jnp.dot(jnp.concatenate(e_rows, axis=0), vcat, preferred_element_type=F32)
                out = pv / jnp.concatenate(den_rows, axis=0)
                for s in range(q_slabs_per_kv):
                    o_ref[r0:r0 + BLOCK, (first_slab + s) * slab:(first_slab + s + 1) * slab] = (
                        out[s * BLOCK:(s + 1) * BLOCK, :].astype(o_ref.dtype))

    _pipelined_steps(i, n_tiles, project, attend)


def _attn_call(l, xb, w_in, b_in, sinks, slopes, *, tm=TILES.mixer_rows):
    s, d = xb.shape
    half = ATTN_WIDTH // 2
    n_tiles = s // tm
    assert GQA_GROUP % 2 == 0 and N_KV_HEADS % 2 == 0
    wq0, bq0 = _in_cols(l, d, COL_Q, half, resident=True)
    wq1, bq1 = _in_cols(l, d, COL_Q + half, half, resident=True)
    wk, bk = _in_cols(l, d, COL_K, KV_WIDTH, resident=True)
    wv, bv = _in_cols(l, d, COL_V, KV_WIDTH, resident=True)
    kv_scratch = pltpu.VMEM((2, tm + BLOCK, N_KV_HEADS * 2 * HEAD_DIM), BF16)
    return pl.pallas_call(
        functools.partial(_attn_kernel, tm=tm, n_tiles=n_tiles),
        out_shape=jax.ShapeDtypeStruct((s, ATTN_WIDTH), BF16),
        grid=(n_tiles + 1,),
        in_specs=[
            pl.BlockSpec((tm, d), lambda i: (jnp.minimum(i, n_tiles - 1), 0)),
            wq0, bq0, wq1, bq1, wk, bk, wv, bv,
            pl.BlockSpec(memory_space=pltpu.SMEM),
            pl.BlockSpec(memory_space=pltpu.SMEM),
        ],
        out_specs=pl.BlockSpec((tm, ATTN_WIDTH), lambda i: (jnp.maximum(i - 1, 0), 0)),
        scratch_shapes=[
            pltpu.VMEM((2, tm, ATTN_WIDTH), BF16),
            kv_scratch, kv_scratch, kv_scratch, kv_scratch,
            pltpu.VMEM((N_Q_HEADS, BLOCK, 2 * BLOCK), F32),
        ],
        compiler_params=_params(1),
        name="attn",
    )(xb, w_in, b_in, w_in, b_in, w_in, b_in, w_in, b_in, sinks, slopes)


def _merge_kernel(x_ref, wga_ref, wgb_ref, wgc_ref, bga_ref, bgb_ref, bgc_ref,
                  ya_ref, yb_ref, yc_ref, wba_ref, wbb_ref, wbc_ref, o_ref, wg_s, wb_s):
    row_starts = (0, CONF_WIDTH, CONF_WIDTH + SC_WIDTH, MIX_WIDTH)

    @pl.when(pl.program_id(1) == 0)
    def _():
        for gi, wg_ref in enumerate((wga_ref, wgb_ref, wgc_ref)):
            wg_s[gi] = wg_ref[...].astype(BF16)
        for bi, wb_ref in enumerate((wba_ref, wbb_ref, wbc_ref)):
            wb_s[row_starts[bi]:row_starts[bi + 1], :] = wb_ref[...].astype(BF16)

    xb = x_ref[...]

    def gated(gi, bg_ref, y_ref):
        gate = jax.nn.sigmoid(jnp.dot(xb, wg_s[gi], preferred_element_type=F32) + bg_ref[...])
        wb = wb_s[row_starts[gi]:row_starts[gi + 1], :]
        return gate * jnp.dot(y_ref[...], wb, preferred_element_type=F32)

    merged = gated(0, bga_ref, ya_ref)
    merged = merged + gated(1, bgb_ref, yb_ref)
    merged = merged + gated(2, bgc_ref, yc_ref)
    o_ref[...] = merged.astype(o_ref.dtype)


def _merge_call(l, xb, w_in, b_in, ya, yb, yc, w_branch, *, tm=TILES.merge_rows, tn=TILES.merge_cols):
    s, d = xb.shape
    nj = d // tn
    assert COL_GATE % tn == 0
    gate0 = COL_GATE // tn
    assert SC_WIDTH == CONF_WIDTH and ATTN_WIDTH == CONF_WIDTH + SC_WIDTH

    n_tiles = s // tm

    def ahead(j, i, lead):
        return jnp.minimum(jnp.where(i >= max(1, n_tiles - lead), j + 1, j), nj - 1)

    def wg_spec(gi):
        return pl.BlockSpec((None, d, tn), lambda j, i: (l, 0, gate0 + gi * nj + ahead(j, i, N_GATES + 1 - gi)))

    def bg_spec(gi):
        return pl.BlockSpec((None, 1, tn), lambda j, i: (l, 0, gate0 + gi * nj + j))

    return pl.pallas_call(
        _merge_kernel,
        out_shape=jax.ShapeDtypeStruct((s, d), BF16),
        grid=(nj, s // tm),
        in_specs=[
            pl.BlockSpec((tm, d), lambda j, i: (i, 0)),
            wg_spec(0), wg_spec(1), wg_spec(2),
            bg_spec(0), bg_spec(1), bg_spec(2),
            pl.BlockSpec((tm, CONF_WIDTH), lambda j, i: (i, 0)),
            pl.BlockSpec((tm, SC_WIDTH), lambda j, i: (i, 0)),
            pl.BlockSpec((tm, ATTN_WIDTH), lambda j, i: (i, 0)),
            pl.BlockSpec((None, CONF_WIDTH, tn), lambda j, i: (l, 0, ahead(j, i, 1))),
            pl.BlockSpec((None, SC_WIDTH, tn), lambda j, i: (l, 1, ahead(j, i, 1))),
            pl.BlockSpec((None, ATTN_WIDTH, tn), lambda j, i: (l, 1, ahead(j, i, 1))),
        ],
        out_specs=pl.BlockSpec((tm, tn), lambda j, i: (i, j)),
        scratch_shapes=[pltpu.VMEM((N_GATES, d, tn), BF16), pltpu.VMEM((MIX_WIDTH, tn), BF16)],
        compiler_params=_params(2),
        name="merge",
    )(xb, w_in, w_in, w_in, b_in, b_in, b_in, ya, yb, yc, w_branch, w_branch, w_branch)


def _out_ln_kernel(a_ref, w_ref, b_ref, res_ref, g_ref, beta_ref, of_ref, ob_ref, z_s, *, nn, n_tiles, alpha):
    i = pl.program_id(0)
    j = pl.program_id(1)
    slab_rows = of_ref.shape[0]

    def matmul_step(par):
        z = jnp.dot(a_ref[...], w_ref[j], preferred_element_type=F32) + b_ref[...]
        z_s[par, j] = z + alpha * res_ref[...]

    def ln_step(par):
        def load_z(rows):
            return jnp.concatenate([z_s[par, jj, rows, :] for jj in range(nn)], axis=-1)

        _ln_row_slab(load_z, pl.multiple_of(j * slab_rows, slab_rows), g_ref, beta_ref, of_ref, ob_ref)

    _pipelined_steps(i, n_tiles, matmul_step, ln_step)


def _out_ln_call(l, a, w_slabs, b, res, g, beta, *, alpha, tm=TILES.out_ln_rows):
    s, kdim = a.shape
    nn, tn = w_slabs.shape[1], w_slabs.shape[3]
    d = nn * tn
    n_tiles = s // tm
    slab_rows = tm // nn
    assert slab_rows % (2 * SUBLANES) == 0

    def tile(i):
        return jnp.minimum(i, n_tiles - 1)

    def col(i, j):
        return jnp.where(i == n_tiles, nn - 1, j)

    def out_slab(i, j):
        return (jnp.where(i == 0, 0, (i - 1) * nn + j), 0)

    return pl.pallas_call(
        functools.partial(_out_ln_kernel, nn=nn, n_tiles=n_tiles, alpha=alpha),
        out_shape=(jax.ShapeDtypeStruct((s, d), F32), jax.ShapeDtypeStruct((s, d), BF16)),
        grid=(n_tiles + 1, nn),
        in_specs=[
            pl.BlockSpec((tm, kdim), lambda i, j: (tile(i), 0)),
            _resident((None, nn, kdim, tn), lambda i, j: (l, 0, 0, 0)),
            pl.BlockSpec((None, 1, tn), lambda i, j: (l, 0, col(i, j))),
            pl.BlockSpec((tm, tn), lambda i, j: (tile(i), col(i, j))),
            pl.BlockSpec((None, 1, d), lambda i, j: (l, 0, 0)),
            pl.BlockSpec((None, 1, d), lambda i, j: (l, 0, 0)),
        ],
        out_specs=(pl.BlockSpec((slab_rows, d), out_slab), pl.BlockSpec((slab_rows, d), out_slab)),
        scratch_shapes=[pltpu.VMEM((2, nn, tm, tn), F32)],
        compiler_params=_params(2),
        name="out_ln",
    )(a, w_slabs, b, res, g, beta)


def _mlp2_ln_kernel(a_ref, w_ref, b_ref, res_ref, g_ref, beta_ref, of_ref, ob_ref, z_s, *, nk, n_tiles, alpha):
    i = pl.program_id(0)
    kk = pl.program_id(1)
    d = z_s.shape[-1]
    slab_rows = of_ref.shape[0]
    slab = pl.ds(pl.multiple_of(kk * slab_rows, slab_rows), slab_rows)
    slab_w = min(MM_LN_SLAB, d)

    def matmul_step(par):
        for n0 in range(0, d, slab_w):
            part = jnp.dot(a_ref[...], w_ref[:, n0:n0 + slab_w].astype(BF16), preferred_element_type=F32)
            z_s[par, :, n0:n0 + slab_w] = jnp.where(kk == 0, part, z_s[par, :, n0:n0 + slab_w] + part)
        z_s[par, slab, :] += alpha * res_ref[...] + b_ref[...]

    def ln_step(par):
        _ln_row_slab(lambda rows: z_s[par, rows, :], pl.multiple_of(kk * slab_rows, slab_rows),
                     g_ref, beta_ref, of_ref, ob_ref)

    @pl.when((i == 0) & (kk == 0))
    def _():
        z_s[...] = jnp.zeros(z_s.shape, F32)

    _pipelined_steps(i, n_tiles, matmul_step, ln_step)


def _mlp2_ln_call(l, a, w, b, res, g, beta, *, alpha, tm=TILES.mlp2_rows, tk=TILES.mlp2_k):
    s, kdim = a.shape
    d = w.shape[-1]
    nk = kdim // tk
    n_tiles = s // tm
    slab_rows = tm // nk
    assert tm % nk == 0 and slab_rows % (2 * SUBLANES) == 0

    def tile(i):
        return jnp.minimum(i, n_tiles - 1)

    def kstep(i, k):
        return jnp.where(i == n_tiles, nk - 1, k)

    def out_slab(i, k):
        return (jnp.where(i == 0, 0, (i - 1) * nk + k), 0)

    return pl.pallas_call(
        functools.partial(_mlp2_ln_kernel, nk=nk, n_tiles=n_tiles, alpha=alpha),
        out_shape=(jax.ShapeDtypeStruct((s, d), F32), jax.ShapeDtypeStruct((s, d), BF16)),
        grid=(n_tiles + 1, nk),
        in_specs=[
            pl.BlockSpec((tm, tk), lambda i, k: (tile(i), kstep(i, k))),
            pl.BlockSpec((None, tk, d), lambda i, k: (l, kstep(i, k), 0)),
            pl.BlockSpec((None, 1, d), lambda i, k: (l, 0, 0)),
            pl.BlockSpec((slab_rows, d), lambda i, k: (tile(i) * nk + kstep(i, k), 0)),
            pl.BlockSpec((None, 1, d), lambda i, k: (l, 0, 0)),
            pl.BlockSpec((None, 1, d), lambda i, k: (l, 0, 0)),
        ],
        out_specs=(pl.BlockSpec((slab_rows, d), out_slab), pl.BlockSpec((slab_rows, d), out_slab)),
        scratch_shapes=[pltpu.VMEM((2, tm, d), F32)],
        compiler_params=_params(2),
        name="mlp2_ln",
    )(a, w, b, res, g, beta)


def _mlp1_kernel(x_ref, w_ref, b_ref, o_ref):
    h = jnp.dot(x_ref[...], w_ref[...].astype(BF16), preferred_element_type=F32) + b_ref[...]
    h = jnp.maximum(h, 0.0)
    o_ref[...] = (h * h).astype(o_ref.dtype)


def _mlp1_call(l, xb, w, b, *, tm=TILES.mlp1_rows, tn=TILES.mlp1_cols):
    s, d = xb.shape
    f = w.shape[-1]
    return pl.pallas_call(
        _mlp1_kernel,
        out_shape=jax.ShapeDtypeStruct((s, f), BF16),
        grid=(s // tm, f // tn),
        in_specs=[
            pl.BlockSpec((tm, d), lambda i, j: (i, 0)),
            pl.BlockSpec((None, d, tn), lambda i, j: (l, 0, j)),
            pl.BlockSpec((None, 1, tn), lambda i, j: (l, 0, j)),
        ],
        out_specs=pl.BlockSpec((tm, tn), lambda i, j: (i, j)),
        compiler_params=_params(2),
        name="mlp1",
    )(xb, w, b)


def kernel(x, w_in, b_in, conf_dw_w, conf_dw_b, conf_ln_g, conf_ln_b, sc_dw_w, attn_sinks, w_branch, w_out, b_out,
           ln1_g, ln1_b, w_mlp1, b_mlp1, w_mlp2, b_mlp2, ln2_g, ln2_b):
    batch, seq, d_model = x.shape
    depth = w_in.shape[0]
    alpha = float((2 * depth) ** 0.25)
    assert batch == 1
    assert w_in.shape[-1] == COL_GATE + N_GATES * d_model

    def row(v):
        return v[:, None, :]

    w_in_b = w_in[:, :, :COL_GATE].astype(BF16)
    out_slab = min(TILES.out_ln_cols, d_model)
    w_out_b = w_out.astype(BF16).reshape(depth, d_model, d_model // out_slab, out_slab).transpose(0, 2, 1, 3)
    b_in_r = row(b_in)
    slopes = jnp.asarray(_alibi_slopes(N_Q_HEADS))

    xf = x.reshape(seq, d_model)
    xb = xf.astype(BF16)
    for l in range(depth):
        ya = _conf_call(l, xb, w_in_b, b_in_r, conf_dw_w, row(conf_dw_b), row(conf_ln_g), row(conf_ln_b))
        yb = _sconv_call(l, xb, w_in_b, b_in_r, sc_dw_w)
        yc = _attn_call(l, xb, w_in_b, b_in_r, attn_sinks[l], slopes)
        merged = _merge_call(l, xb, w_in, b_in_r, ya, yb, yc, w_branch)
        xf, xb = _out_ln_call(l, merged, w_out_b, row(b_out), xf, row(ln1_g), row(ln1_b), alpha=alpha)
        hid = _mlp1_call(l, xb, w_mlp1, row(b_mlp1))
        xf, xb = _mlp2_ln_call(l, hid, w_mlp2, row(b_mlp2), xf, row(ln2_g), row(ln2_b), alpha=alpha)
    return xf.reshape(batch, seq, d_model)
```
